```python
import jax, jax.numpy as jnp
from jax import lax
import numpy as np

D_MODEL = 2048
BATCH = 8
SEQ = 4096
DEPTH = 2

EPS = 1e-6
N_BRANCH = 3
CONV_WIDTH = D_MODEL
CONV_K = 3
SB_HEAD_DIM = 128
SB_HEADS = D_MODEL // SB_HEAD_DIM
SB_WIDTH = SB_HEADS * SB_HEAD_DIM
Q_BLOCK = 128
SSD_EXPAND = 2
SSD_WIDTH = SSD_EXPAND * D_MODEL
SSD_HEAD_DIM = 64
SSD_HEADS = SSD_WIDTH // SSD_HEAD_DIM
SSD_GROUPS = 8
SSD_HPG = SSD_HEADS // SSD_GROUPS
SSD_STATE = 128
SSD_CONV_K = 4
SSD_CHUNK = 128
SSD_XBC = SSD_WIDTH + 2 * SSD_GROUPS * SSD_STATE
SSD_IN = SSD_WIDTH + SSD_XBC + SSD_HEADS
A_IN = 3 * CONV_WIDTH
B_IN = 3 * SB_WIDTH
G_IN = N_BRANCH * D_MODEL
IN_PROJ_WIDTH = A_IN + B_IN + SSD_IN + G_IN
PEER_HEADS = 8
PEER_NKEYS = 128
PEER_EXPERTS = PEER_NKEYS * PEER_NKEYS
PEER_TOPK = 16
PEER_DK = 256
PEER_TOKEN_BLOCK = 128

kernel_name = "hybrid_conv_stickbreak_ssd_peer"


def rmsnorm(x, w):
    xf = x.astype(jnp.float32)
    y = xf * lax.rsqrt(jnp.mean(xf * xf, axis=-1, keepdims=True) + EPS)
    return (y * w.astype(jnp.float32)).astype(x.dtype)


def causal_depthwise_conv(x, w, b=None):
    k_width, chans = w.shape
    y = lax.conv_general_dilated(
        x, w[:, None, :].astype(x.dtype), window_strides=(1,),
        padding=[(k_width - 1, 0)], dimension_numbers=('NWC', 'WIO', 'NWC'),
        feature_group_count=chans)
    return y if b is None else y + b.astype(x.dtype)


def short_conv_mixer(u, conv_w):
    b_gate, c_gate, hid = jnp.split(u, 3, axis=-1)
    return b_gate * causal_depthwise_conv(c_gate * hid, conv_w)


def stick_breaking_attention(u):
    q, k, v = jnp.split(u, 3, axis=-1)
    bsz, seq, _ = q.shape
    q = q.reshape(bsz, seq, SB_HEADS, SB_HEAD_DIM)
    k = k.reshape(bsz, seq, SB_HEADS, SB_HEAD_DIM)
    v = v.reshape(bsz, seq, SB_HEADS, SB_HEAD_DIM)
    scale = SB_HEAD_DIM ** -0.5
    key_pos = jnp.arange(seq)

    def block(i):
        start = i * Q_BLOCK
        qb = lax.dynamic_slice_in_dim(q, start, Q_BLOCK, axis=1)
        z = jnp.einsum('bthd,bshd->bhts', qb, k, preferred_element_type=jnp.float32) * scale
        q_pos = start + jnp.arange(Q_BLOCK)
        mask = key_pos[None, :] < q_pos[:, None]
        log_not_beta = jnp.where(mask, jax.nn.log_sigmoid(-z), 0.0)
        later = lax.cumsum(log_not_beta, axis=3, reverse=True) - log_not_beta
        a = jnp.where(mask, jnp.exp(jax.nn.log_sigmoid(z) + later), 0.0)
        return jnp.einsum('bhts,bshd->bthd', a.astype(v.dtype), v)

    o = lax.map(block, jnp.arange(seq // Q_BLOCK))
    return o.transpose(1, 0, 2, 3, 4).reshape(bsz, seq, SB_WIDTH)


def segsum_exp(a):
    length = a.shape[-1]
    cs = jnp.cumsum(a, axis=-1)
    diff = cs[..., :, None] - cs[..., None, :]
    mask = jnp.tril(jnp.ones((length, length), dtype=bool))
    return jnp.where(mask, jnp.exp(jnp.where(mask, diff, 0.0)), 0.0)


def mamba2_ssd(u, conv_w, conv_b, dt_bias, a_log, d_skip, norm_w):
    bsz, seq, _ = u.shape
    nc = seq // SSD_CHUNK
    z, xbc, dt = jnp.split(u, [SSD_WIDTH, SSD_WIDTH + SSD_XBC], axis=-1)
    xbc = jax.nn.silu(causal_depthwise_conv(xbc, conv_w, conv_b))
    xs, bm, cm = jnp.split(xbc, [SSD_WIDTH, SSD_WIDTH + SSD_GROUPS * SSD_STATE], axis=-1)
    xs = xs.astype(jnp.float32).reshape(bsz, nc, SSD_CHUNK, SSD_GROUPS, SSD_HPG, SSD_HEAD_DIM)
    bm = bm.astype(jnp.float32).reshape(bsz, nc, SSD_CHUNK, SSD_GROUPS, SSD_STATE)
    cm = cm.astype(jnp.float32).reshape(bsz, nc, SSD_CHUNK, SSD_GROUPS, SSD_STATE)
    dt = jax.nn.softplus(dt.astype(jnp.float32) + dt_bias.astype(jnp.float32))
    a = -jnp.exp(a_log.astype(jnp.float32))
    dt_c = dt.reshape(bsz, nc, SSD_CHUNK, SSD_GROUPS, SSD_HPG)
    da = (dt_c * a.reshape(SSD_GROUPS, SSD_HPG)).transpose(0, 3, 4, 1, 2)
    xdt = xs * dt_c[..., None]
    a_cs = jnp.cumsum(da, axis=-1)
    cb = jnp.einsum('bclgn,bcsgn->bgcls', cm, bm)
    y_diag = jnp.einsum('bgcls,bgrcls,bcsgrp->bclgrp', cb, segsum_exp(da), xdt)
    decay_to_end = jnp.exp(a_cs[..., -1:] - a_cs)
    states = jnp.einsum('bclgn,bgrcl,bclgrp->bcgrpn', bm, decay_to_end, xdt)
    chunk_decay = jnp.exp(a_cs[..., -1])

    def step(hs, inp):
        s_c, d_c = inp
        return hs * d_c[..., None, None] + s_c, hs

    h0 = jnp.zeros((bsz, SSD_GROUPS, SSD_HPG, SSD_HEAD_DIM, SSD_STATE), jnp.float32)
    _, prev = lax.scan(step, h0, (states.transpose(1, 0, 2, 3, 4, 5), chunk_decay.transpose(3, 0, 1, 2)))
    y_off = jnp.einsum('bclgn,cbgrpn,bgrcl->bclgrp', cm, prev, jnp.exp(a_cs))
    y = y_diag + y_off + xs * d_skip.astype(jnp.float32).reshape(SSD_GROUPS, SSD_HPG)[:, :, None]
    y = y.reshape(bsz, seq, SSD_WIDTH)
    yg = (y * jax.nn.silu(z.astype(jnp.float32))).reshape(bsz, seq, SSD_GROUPS, -1)
    yg = yg * lax.rsqrt(jnp.mean(yg * yg, axis=-1, keepdims=True) + EPS)
    return (yg.reshape(bsz, seq, SSD_WIDTH) * norm_w.astype(jnp.float32)).astype(u.dtype)


def hybrid_mixer(n, w_in, conv_a_w, ssd_conv_w, ssd_conv_b, ssd_dt_bias, ssd_a_log, ssd_d,
                 ssd_norm_w, w_branch_a, w_branch_b, w_branch_c, w_out):
    u = n @ w_in
    u_a, u_b, u_c, u_g = jnp.split(u, [A_IN, A_IN + B_IN, A_IN + B_IN + SSD_IN], axis=-1)
    y_a = short_conv_mixer(u_a, conv_a_w) @ w_branch_a
    y_b = stick_breaking_attention(u_b) @ w_branch_b
    y_c = mamba2_ssd(u_c, ssd_conv_w, ssd_conv_b, ssd_dt_bias, ssd_a_log, ssd_d, ssd_norm_w) @ w_branch_c
    g_a, g_b, g_c = jnp.split(jax.nn.sigmoid(u_g), 3, axis=-1)
    return (g_a * y_a + g_b * y_b + g_c * y_c) @ w_out


def peer_ffn(n, w_query, sub_keys, expert_down, expert_up):
    bsz, seq, dm = n.shape
    ntok = bsz * seq
    nt = n.reshape(ntok, dm)
    q = (nt @ w_query).reshape(ntok, PEER_HEADS, 2, PEER_DK // 2)
    s = jnp.einsum('thcd,hcnd->thcn', q, sub_keys, preferred_element_type=jnp.float32)
    s1, i1 = lax.top_k(s[:, :, 0], PEER_TOPK)
    s2, i2 = lax.top_k(s[:, :, 1], PEER_TOPK)
    cand_s = (s1[..., :, None] + s2[..., None, :]).reshape(ntok, PEER_HEADS, PEER_TOPK * PEER_TOPK)
    cand_i = (i1[..., :, None] * PEER_NKEYS + i2[..., None, :]).reshape(ntok, PEER_HEADS, PEER_TOPK * PEER_TOPK)
    top_s, pos = lax.top_k(cand_s, PEER_TOPK)
    idx = jnp.take_along_axis(cand_i, pos, axis=-1)
    gate = jax.nn.softmax(top_s, axis=-1)
    nb = ntok // PEER_TOKEN_BLOCK
    hk = PEER_HEADS * PEER_TOPK

    def block(args):
        xb, ib, gb = args
        down = expert_down[ib]
        act = jax.nn.gelu(jnp.einsum('td,tkd->tk', xb, down), approximate=False)
        up = expert_up[ib]
        return jnp.einsum('tk,tkd->td', gb.astype(xb.dtype) * act, up)

    out = lax.map(block, (nt.reshape(nb, PEER_TOKEN_BLOCK, dm),
                          idx.reshape(nb, PEER_TOKEN_BLOCK, hk),
                          gate.reshape(nb, PEER_TOKEN_BLOCK, hk)))
    return out.reshape(bsz, seq, dm)


def setup_inputs(seed: int = 0) -> dict:
    key = jax.random.key(seed)
    ks = jax.random.split(key, 24)
    f32 = jnp.float32

    def nrm(k, shape, scale):
        return jax.random.normal(k, shape, f32) * scale

    dt0 = jnp.exp(jax.random.uniform(ks[8], (DEPTH, SSD_HEADS), f32, np.log(1e-3), np.log(1e-1)))
    return {
        "x": nrm(ks[0], (BATCH, SEQ, D_MODEL), 1.0),
        "norm_mix_w": 1.0 + nrm(ks[1], (DEPTH, D_MODEL), 0.02),
        "w_in": nrm(ks[2], (DEPTH, D_MODEL, IN_PROJ_WIDTH), D_MODEL ** -0.5),
        "conv_a_w": nrm(ks[3], (DEPTH, CONV_K, CONV_WIDTH), CONV_K ** -0.5),
        "ssd_conv_w": nrm(ks[4], (DEPTH, SSD_CONV_K, SSD_XBC), SSD_CONV_K ** -0.5),
        "ssd_conv_b": nrm(ks[5], (DEPTH, SSD_XBC), 0.01),
        "ssd_dt_bias": dt0 + jnp.log(-jnp.expm1(-dt0)),
        "ssd_a_log": jnp.log(jax.random.uniform(ks[6], (DEPTH, SSD_HEADS), f32, 1.0, 16.0)),
        "ssd_d": 1.0 + nrm(ks[7], (DEPTH, SSD_HEADS), 0.02),
        "ssd_norm_w": 1.0 + nrm(ks[9], (DEPTH, SSD_WIDTH), 0.02),
        "w_branch_a": nrm(ks[10], (DEPTH, CONV_WIDTH, D_MODEL), CONV_WIDTH ** -0.5),
        "w_branch_b": nrm(ks[11], (DEPTH, SB_WIDTH, D_MODEL), SB_WIDTH ** -0.5),
        "w_branch_c": nrm(ks[12], (DEPTH, SSD_WIDTH, D_MODEL), SSD_WIDTH ** -0.5),
        "w_out": nrm(ks[13], (DEPTH, D_MODEL, D_MODEL), D_MODEL ** -0.5),
        "norm_ffn_w": 1.0 + nrm(ks[14], (DEPTH, D_MODEL), 0.02),
        "peer_w_query": nrm(ks[15], (DEPTH, D_MODEL, PEER_HEADS * PEER_DK), D_MODEL ** -0.5),
        "peer_sub_keys": nrm(ks[16], (DEPTH, PEER_HEADS, 2, PEER_NKEYS, PEER_DK // 2), (PEER_DK // 2) ** -0.5),
        "peer_down": nrm(ks[17], (DEPTH, PEER_EXPERTS, D_MODEL), D_MODEL ** -0.5),
        "peer_up": nrm(ks[18], (DEPTH, PEER_EXPERTS, D_MODEL), PEER_HEADS ** -0.5),
        "final_norm_w": 1.0 + nrm(ks[19], (D_MODEL,), 0.02),
    }


def reference(x, norm_mix_w, w_in, conv_a_w, ssd_conv_w, ssd_conv_b, ssd_dt_bias, ssd_a_log,
              ssd_d, ssd_norm_w, w_branch_a, w_branch_b, w_branch_c, w_out, norm_ffn_w,
              peer_w_query, peer_sub_keys, peer_down, peer_up, final_norm_w):
    h = x
    for l in range(DEPTH):
        h = h + hybrid_mixer(rmsnorm(h, norm_mix_w[l]), w_in[l], conv_a_w[l], ssd_conv_w[l],
                             ssd_conv_b[l], ssd_dt_bias[l], ssd_a_log[l], ssd_d[l], ssd_norm_w[l],
                             w_branch_a[l], w_branch_b[l], w_branch_c[l], w_out[l])
        h = h + peer_ffn(rmsnorm(h, norm_ffn_w[l]), peer_w_query[l], peer_sub_keys[l],
                         peer_down[l], peer_up[l])
    return rmsnorm(h, final_norm_w)
```

```python
import functools

import jax
import jax.numpy as jnp
from jax import lax
from jax.experimental import pallas as pl
from jax.experimental.pallas import tpu as pltpu

F32 = jnp.float32
BF16 = jnp.bfloat16
I32 = jnp.int32

EPS = 1e-6
SB_HEAD_DIM = 128
SSD_HEAD_DIM = 64
SSD_GROUPS = 8
SSD_STATE = 128
SSD_CHUNK = 128
PEER_HEADS = 8
PEER_TOPK = 16

LANES = 128
SUBLANES = 8
VMEM_LIMIT = 48 * 1024 * 1024
VMEM_LIMIT_FFN = 56 * 1024 * 1024


def _params(*sem, vmem=VMEM_LIMIT):
    return pltpu.CompilerParams(dimension_semantics=sem, vmem_limit_bytes=vmem)


def _log2(n):
    assert n & (n - 1) == 0
    return n.bit_length() - 1


def _tile(n, pref):
    if n <= pref:
        return n
    t = pref
    while n % t:
        t -= SUBLANES
    return t


def _rmsnorm_kernel(x_ref, w_ref, o_ref):
    x = x_ref[...].astype(F32)
    ms = jnp.mean(x * x, axis=-1, keepdims=True)
    o_ref[...] = (x * lax.rsqrt(ms + EPS) * w_ref[...]).astype(o_ref.dtype)


def _rmsnorm(x, w, out_dtype):
    t, d = x.shape
    tm = _tile(t, 512)
    return pl.pallas_call(
        _rmsnorm_kernel,
        grid=(t // tm,),
        in_specs=[pl.BlockSpec((tm, d), lambda i: (i, 0)),
                  pl.BlockSpec((1, d), lambda i: (0, 0))],
        out_specs=pl.BlockSpec((tm, d), lambda i: (i, 0)),
        out_shape=jax.ShapeDtypeStruct((t, d), out_dtype),
        compiler_params=_params("parallel"),
    )(x, w.reshape(1, d).astype(F32))


def _mm_kernel(*refs, has_gate, has_add):
    x_ref, w_ref = refs[0], refs[1]
    o_ref = refs[-1]
    acc = jnp.dot(x_ref[...], w_ref[...], preferred_element_type=F32)
    k = 2
    if has_gate:
        acc = acc * jax.nn.sigmoid(refs[k][...].astype(F32))
        k += 1
    if has_add:
        acc = acc + refs[k][...].astype(F32)
    o_ref[...] = acc.astype(o_ref.dtype)


def _matmul(x, w, out_dtype, gate=None, gate_col=0, add=None, tm=1024, tn=1024):
    m, kd = x.shape
    n = w.shape[1]
    tm = _tile(m, tm)
    tn = min(tn, n)
    assert n % tn == 0 and gate_col % tn == 0
    goff = gate_col // tn
    ins = [x, w]
    specs = [pl.BlockSpec((tm, kd), lambda i, j: (i, 0)),
             pl.BlockSpec((kd, tn), lambda i, j: (0, j))]
    if gate is not None:
        ins.append(gate)
        specs.append(pl.BlockSpec((tm, tn), lambda i, j: (i, goff + j)))
    if add is not None:
        ins.append(add)
        specs.append(pl.BlockSpec((tm, tn), lambda i, j: (i, j)))
    return pl.pallas_call(
        functools.partial(_mm_kernel, has_gate=gate is not None, has_add=add is not None),
        grid=(m // tm, n // tn),
        in_specs=specs,
        out_specs=pl.BlockSpec((tm, tn), lambda i, j: (i, j)),
        out_shape=jax.ShapeDtypeStruct((m, n), out_dtype),
        compiler_params=_params("parallel", "parallel"),
    )(*ins)


def _conv_taps(cur, buf_ref, w_ref, first):
    ts = cur.shape[0]
    kw = w_ref.shape[0]

    @pl.when(first)
    def _():
        buf_ref[0:SUBLANES, :] = jnp.zeros((SUBLANES, cur.shape[1]), F32)

    buf_ref[SUBLANES:SUBLANES + ts, :] = cur
    y = cur * w_ref[kw - 1:kw, :]
    for d in range(1, kw):
        y = y + buf_ref[SUBLANES - d:SUBLANES - d + ts, :] * w_ref[kw - 1 - d:kw - d, :]
    buf_ref[0:SUBLANES, :] = cur[ts - SUBLANES:ts, :]
    return y


def _short_conv_kernel(b_ref, c_ref, h_ref, w_ref, o_ref, buf_ref):
    cur = c_ref[...].astype(F32) * h_ref[...].astype(F32)
    y = _conv_taps(cur, buf_ref, w_ref, pl.program_id(2) == 0)
    o_ref[...] = (b_ref[...].astype(F32) * y).astype(o_ref.dtype)


def _short_conv(u_a, conv_w, bsz, seq):
    t = u_a.shape[0]
    width = conv_w.shape[1]
    ts = _tile(seq, 512)
    tc = 512
    nc = width // tc
    ns = seq // ts
    row = lambda b, j, s: b * ns + s
    return pl.pallas_call(
        _short_conv_kernel,
        grid=(bsz, nc, ns),
        in_specs=[pl.BlockSpec((ts, tc), lambda b, j, s: (row(b, j, s), j)),
                  pl.BlockSpec((ts, tc), lambda b, j, s: (row(b, j, s), nc + j)),
                  pl.BlockSpec((ts, tc), lambda b, j, s: (row(b, j, s), 2 * nc + j)),
                  pl.BlockSpec((conv_w.shape[0], tc), lambda b, j, s: (0, j))],
        out_specs=pl.BlockSpec((ts, tc), lambda b, j, s: (row(b, j, s), j)),
        out_shape=jax.ShapeDtypeStruct((t, width), BF16),
        scratch_shapes=[pltpu.VMEM((SUBLANES + ts, tc), F32)],
        compiler_params=_params("parallel", "parallel", "arbitrary"),
    )(u_a, u_a, u_a, conv_w.astype(F32))


def _ssd_conv_kernel(x_ref, w_ref, b_ref, o_ref, buf_ref):
    y = _conv_taps(x_ref[...].astype(F32), buf_ref, w_ref, pl.program_id(2) == 0) + b_ref[...]
    o_ref[...] = (y * jax.nn.sigmoid(y)).astype(o_ref.dtype)


def _ssd_conv(xbc, conv_w, conv_b, bsz, seq):
    t, width = xbc.shape
    ts = _tile(seq, 512)
    tc = 512
    nc = width // tc
    ns = seq // ts
    return pl.pallas_call(
        _ssd_conv_kernel,
        grid=(bsz, nc, ns),
        in_specs=[pl.BlockSpec((ts, tc), lambda b, j, s: (b * ns + s, j)),
                  pl.BlockSpec((conv_w.shape[0], tc), lambda b, j, s: (0, j)),
                  pl.BlockSpec((1, tc), lambda b, j, s: (0, j))],
        out_specs=pl.BlockSpec((ts, tc), lambda b, j, s: (b * ns + s, j)),
        out_shape=jax.ShapeDtypeStruct((t, width), BF16),
        scratch_shapes=[pltpu.VMEM((SUBLANES + ts, tc), F32)],
        compiler_params=_params("parallel", "parallel", "arbitrary"),
    )(xbc, conv_w.astype(F32), conv_b.reshape(1, width).astype(F32))


def _split2(x):
    hi = x.astype(BF16)
    lo = (x - hi.astype(F32)).astype(BF16)
    return hi, lo


def _split3(x):
    hi = x.astype(BF16)
    r = x - hi.astype(F32)
    mid = r.astype(BF16)
    lo = (r - mid.astype(F32)).astype(BF16)
    return hi, mid, lo


def _sb_attn_kernel(q_ref, k_ref, v_ref, o_ref, *, blk, scale):
    i = pl.program_id(2)
    q = q_ref[...]
    row = lax.broadcasted_iota(I32, (blk, blk), 0)
    col = lax.broadcasted_iota(I32, (blk, blk), 1)
    later_mat = jnp.where(row > col, 1.0, 0.0).astype(BF16)

    def body(jj, carry):
        o_acc, run = carry
        j = i - jj
        start = pl.multiple_of(j * blk, blk)
        ks = k_ref[pl.ds(start, blk), :]
        vs = v_ref[pl.ds(start, blk), :]
        z = lax.dot_general(q, ks, (((1,), (1,)), ((), ())), preferred_element_type=F32) * scale
        mask = col < row + jj * blk
        ls = jnp.minimum(z, 0.0) - jnp.log1p(jnp.exp(-jnp.abs(z)))
        lnb = jnp.where(mask, ls - z, 0.0)
        hi, lo = _split2(lnb)
        later = (jnp.dot(hi, later_mat, preferred_element_type=F32)
                 + jnp.dot(lo, later_mat, preferred_element_type=F32))
        a = jnp.where(mask, jnp.exp(ls + later + run), 0.0)
        o_acc = o_acc + jnp.dot(a.astype(BF16), vs, preferred_element_type=F32)
        run = run + jnp.sum(lnb, axis=1, keepdims=True)
        return o_acc, run

    o_acc, _ = lax.fori_loop(
        0, i + 1, body,
        (jnp.zeros((blk, q.shape[1]), F32), jnp.zeros((blk, 1), F32)))
    o_ref[...] = o_acc.astype(o_ref.dtype)


def _sb_attention(u_b, bsz, seq):
    t, w3 = u_b.shape
    width = w3 // 3
    heads = width // SB_HEAD_DIM
    blk = _tile(seq, 256)
    nq = seq // blk
    return pl.pallas_call(
        functools.partial(_sb_attn_kernel, blk=blk, scale=SB_HEAD_DIM ** -0.5),
        grid=(bsz, heads, nq),
        in_specs=[pl.BlockSpec((blk, SB_HEAD_DIM), lambda b, h, i: (b * nq + i, h)),
                  pl.BlockSpec((seq, SB_HEAD_DIM), lambda b, h, i: (b, heads + h)),
                  pl.BlockSpec((seq, SB_HEAD_DIM), lambda b, h, i: (b, 2 * heads + h))],
        out_specs=pl.BlockSpec((blk, SB_HEAD_DIM), lambda b, h, i: (b * nq + i, h)),
        out_shape=jax.ShapeDtypeStruct((t, width), BF16),
        compiler_params=_params("parallel", "parallel", "arbitrary"),
    )(u_b, u_b, u_b)


def _ssd_kernel(xs_ref, bm_ref, cm_ref, z_ref, dt_ref, dtb_ref, alog_ref, dsk_ref, nw_ref,
                o_ref, state_ref, acst_ref, *, hpg, hd):
    g = pl.program_id(1)
    c = pl.program_id(2)
    lc = xs_ref.shape[0]
    gw = hpg * hd

    @pl.when(c == 0)
    def _():
        state_ref[...] = jnp.zeros(state_ref.shape, F32)

    dt = jax.nn.softplus(dt_ref[...] + dtb_ref[...])
    da = dt * (-jnp.exp(alog_ref[...]))
    row = lax.broadcasted_iota(I32, (lc, lc), 0)
    col = lax.broadcasted_iota(I32, (lc, lc), 1)
    causal = row >= col
    tri = jnp.where(causal, 1.0, 0.0).astype(BF16)
    a_cs = sum(jnp.dot(tri, part, preferred_element_type=F32) for part in _split3(da))
    acst_ref[...] = a_cs.T
    exp_acs = jnp.exp(a_cs)
    dte = jnp.exp(a_cs[lc - 1:lc, :] - a_cs)

    hrow = lax.broadcasted_iota(I32, (LANES, gw), 0)
    hcol = lax.broadcasted_iota(I32, (LANES, gw), 1)
    spread = jnp.where(hrow == g * hpg + (hcol >> _log2(hd)), 1.0, 0.0).astype(BF16)
    hrow2 = lax.broadcasted_iota(I32, (LANES, hpg * lc), 0)
    hcol2 = lax.broadcasted_iota(I32, (LANES, hpg * lc), 1)
    spread_l = jnp.where(hrow2 == g * hpg + (hcol2 >> _log2(lc)), 1.0, 0.0).astype(BF16)

    def expand(x, parts, mat):
        pieces = (_split3(x) if parts == 3 else _split2(x) if parts == 2 else (x.astype(BF16),))
        return sum(jnp.dot(p, mat, preferred_element_type=F32) for p in pieces)

    dt_x = expand(dt, 2, spread)
    dte_x = expand(dte, 2, spread)
    eacs_x = expand(exp_acs, 2, spread)
    dsk_x = expand(jnp.broadcast_to(dsk_ref[...], (SUBLANES, LANES)), 3, spread)[0:1, :]
    acs_col = expand(a_cs, 3, spread_l)

    xs = xs_ref[...].astype(F32)
    bm = bm_ref[...]
    cm = cm_ref[...]
    xdt = xs * dt_x
    xdt_b = xdt.astype(BF16)
    cb = lax.dot_general(cm, bm, (((1,), (1,)), ((), ())), preferred_element_type=F32)

    y_parts = []
    for r in range(hpg):
        a_l = acs_col[:, r * lc:(r + 1) * lc]
        a_s = acst_ref[pl.ds(g * hpg + r, 1), :]
        seg = jnp.where(causal, jnp.exp(jnp.where(causal, a_l - a_s, 0.0)), 0.0)
        m = (cb * seg).astype(BF16)
        y_parts.append(jnp.dot(m, xdt_b[:, r * hd:(r + 1) * hd], preferred_element_type=F32))
    y_diag = jnp.concatenate(y_parts, axis=1)

    prev = state_ref[...]
    y_off = jnp.dot(cm, prev.astype(BF16), preferred_element_type=F32) * eacs_x
    y = y_diag + y_off + xs * dsk_x

    bm_t = bm.astype(F32).T.astype(BF16)
    new_state = jnp.dot(bm_t, (xdt * dte_x).astype(BF16), preferred_element_type=F32)
    state_ref[...] = prev * eacs_x[lc - 1:lc, :] + new_state

    zf = z_ref[...].astype(F32)
    yg = y * (zf * jax.nn.sigmoid(zf))
    ms = jnp.mean(yg * yg, axis=-1, keepdims=True)
    o_ref[...] = (yg * lax.rsqrt(ms + EPS) * nw_ref[...]).astype(o_ref.dtype)


def _pad_lanes(v):
    return jnp.zeros((1, LANES), F32).at[0, :v.shape[0]].set(v.astype(F32))


def _ssd(xbc_act, z, dt_raw, dt_bias, a_log, d_skip, norm_w, bsz, seq):
    t, width = z.shape
    heads = width // SSD_HEAD_DIM
    hpg = heads // SSD_GROUPS
    gw = hpg * SSD_HEAD_DIM
    lc = SSD_CHUNK
    nch = seq // lc
    nxg = width // gw
    nbs = width // SSD_STATE
    assert gw * SSD_GROUPS == width and SSD_STATE == LANES and heads <= LANES
    row = lambda b, g, c: b * nch + c
    return pl.pallas_call(
        functools.partial(_ssd_kernel, hpg=hpg, hd=SSD_HEAD_DIM),
        grid=(bsz, SSD_GROUPS, nch),
        in_specs=[pl.BlockSpec((lc, gw), lambda b, g, c: (row(b, g, c), g)),
                  pl.BlockSpec((lc, SSD_STATE), lambda b, g, c: (row(b, g, c), nbs + g)),
                  pl.BlockSpec((lc, SSD_STATE), lambda b, g, c: (row(b, g, c), nbs + SSD_GROUPS + g)),
                  pl.BlockSpec((lc, gw), lambda b, g, c: (row(b, g, c), g)),
                  pl.BlockSpec((lc, LANES), lambda b, g, c: (row(b, g, c), 0)),
                  pl.BlockSpec((1, LANES), lambda b, g, c: (0, 0)),
                  pl.BlockSpec((1, LANES), lambda b, g, c: (0, 0)),
                  pl.BlockSpec((1, LANES), lambda b, g, c: (0, 0)),
                  pl.BlockSpec((1, gw), lambda b, g, c: (0, g))],
        out_specs=pl.BlockSpec((lc, gw), lambda b, g, c: (row(b, g, c), g)),
        out_shape=jax.ShapeDtypeStruct((t, width), BF16),
        scratch_shapes=[pltpu.VMEM((SSD_STATE, gw), F32), pltpu.VMEM((LANES, lc), F32)],
        compiler_params=_params("parallel", "parallel", "arbitrary"),
    )(xbc_act, xbc_act, xbc_act, z, dt_raw, _pad_lanes(dt_bias), _pad_lanes(a_log),
      _pad_lanes(d_skip), norm_w.reshape(1, width).astype(F32))


def _topk_rows(s, k):
    n = s.shape[0]
    iota = lax.broadcasted_iota(I32, s.shape, 0)
    vals, idxs = [], []
    for _ in range(k):
        m = jnp.max(s, axis=0, keepdims=True)
        i = jnp.min(jnp.where(s == m, iota, n), axis=0, keepdims=True)
        vals.append(m)
        idxs.append(i)
        s = jnp.where(iota == i, -jnp.inf, s)
    return jnp.concatenate(vals, axis=0), jnp.concatenate(idxs, axis=0)


def _peer_topk_kernel(q_ref, keys_ref, i1_ref, i2_ref, g_ref, *, topk):
    half = q_ref.shape[1] // 2
    sub_v, sub_i = [], []
    for c in range(2):
        s = lax.dot_general(keys_ref[c], q_ref[:, c * half:(c + 1) * half],
                            (((1,), (1,)), ((), ())), preferred_element_type=F32)
        v, i = _topk_rows(s, topk)
        sub_v.append(v)
        sub_i.append(i)
    cand = jnp.concatenate([sub_v[0][a:a + 1, :] + sub_v[1] for a in range(topk)], axis=0)
    top_s, pos = _topk_rows(cand, topk)
    pa = pos >> _log2(topk)
    pb = pos & (topk - 1)
    i1 = jnp.zeros(pos.shape, I32)
    i2 = jnp.zeros(pos.shape, I32)
    for a in range(topk):
        i1 = jnp.where(pa == a, sub_i[0][a:a + 1, :], i1)
        i2 = jnp.where(pb == a, sub_i[1][a:a + 1, :], i2)
    e = jnp.exp(top_s - top_s[0:1, :])
    i1_ref[...] = i1
    i2_ref[...] = i2
    g_ref[...] = e / jnp.sum(e, axis=0, keepdims=True)


def _peer_topk(q, sub_keys):
    t, qw = q.shape
    heads, _, nkeys, dk2 = sub_keys.shape
    tt = _tile(t, 256)
    hk = heads * PEER_TOPK
    out = jax.ShapeDtypeStruct((hk, t), I32)
    spec = pl.BlockSpec((PEER_TOPK, tt), lambda i, h: (h, i))
    return pl.pallas_call(
        functools.partial(_peer_topk_kernel, topk=PEER_TOPK),
        grid=(t // tt, heads),
        in_specs=[pl.BlockSpec((tt, 2 * dk2), lambda i, h: (i, h)),
                  pl.BlockSpec((None, 2, nkeys, dk2), lambda i, h: (h, 0, 0, 0))],
        out_specs=[spec, spec, spec],
        out_shape=[out, out, jax.ShapeDtypeStruct((hk, t), F32)],
        compiler_params=_params("parallel", "parallel"),
    )(q, sub_keys.astype(BF16))


def _peer_gates_kernel(i1_ref, i2_ref, g_ref, o_ref, i1s_ref, i2s_ref, gs_ref):
    nk = o_ref.shape[1]
    i1s_ref[...] = i1_ref[...].astype(F32).T
    i2s_ref[...] = i2_ref[...].astype(F32).T
    gs_ref[...] = g_ref[...].T
    key = lax.broadcasted_iota(I32, (nk, i1_ref.shape[0]), 0).astype(F32)

    def body(t, _):
        r1 = i1s_ref[pl.ds(t, 1), :]
        r2 = i2s_ref[pl.ds(t, 1), :]
        gg = gs_ref[pl.ds(t, 1), :]
        ghi = gg.astype(BF16).astype(F32)
        glo = gg - ghi
        eq1 = key == r1
        a_mat = jnp.concatenate([jnp.where(eq1, ghi, 0.0), jnp.where(eq1, glo, 0.0)],
                                axis=1).astype(BF16)
        b_one = jnp.where(key == r2, 1.0, 0.0)
        b_mat = jnp.concatenate([b_one, b_one], axis=1).astype(BF16)
        o_ref[t] = lax.dot_general(a_mat, b_mat, (((1,), (1,)), ((), ())),
                                   preferred_element_type=F32)
        return 0

    lax.fori_loop(0, o_ref.shape[0], body, 0)


def _peer_gates(i1t, i2t, gt, nkeys):
    hk, t = i1t.shape
    tt = _tile(t, 128)
    spec = pl.BlockSpec((hk, tt), lambda i: (0, i))
    return pl.pallas_call(
        _peer_gates_kernel,
        grid=(t // tt,),
        in_specs=[spec, spec, spec],
        out_specs=pl.BlockSpec((tt, nkeys, nkeys), lambda i: (i, 0, 0)),
        out_shape=jax.ShapeDtypeStruct((t, nkeys, nkeys), F32),
        scratch_shapes=[pltpu.VMEM((tt, hk), F32)] * 3,
        compiler_params=_params("parallel"),
    )(i1t, i2t, gt)


def _peer_ffn_kernel(x_ref, dn_ref, up_ref, gd_ref, h_ref, o_ref, acc_ref):
    j = pl.program_id(1)

    @pl.when(j == 0)
    def _():
        acc_ref[...] = jnp.zeros(acc_ref.shape, F32)

    s = jnp.dot(x_ref[...], dn_ref[...], preferred_element_type=F32)
    nk = gd_ref.shape[2]
    pieces = []
    for c in range(gd_ref.shape[1]):
        sc = s[:, c * nk:(c + 1) * nk]
        act = 0.5 * sc * (1.0 + lax.erf(sc * (2.0 ** -0.5)))
        pieces.append((act * gd_ref[:, c, :]).astype(BF16))
    a = jnp.concatenate(pieces, axis=1)
    acc_ref[...] += jnp.dot(a, up_ref[...], preferred_element_type=F32)

    @pl.when(j == pl.num_programs(1) - 1)
    def _():
        o_ref[...] = h_ref[...] + acc_ref[...]


def _peer_ffn(n2, down_t, up, gd3, h):
    t, d = n2.shape
    ne = up.shape[0]
    nk = gd3.shape[2]
    tm = _tile(t, 512)
    te = SUBLANES * nk
    return pl.pallas_call(
        _peer_ffn_kernel,
        grid=(t // tm, ne // te),
        in_specs=[pl.BlockSpec((tm, d), lambda i, j: (i, 0)),
                  pl.BlockSpec((d, te), lambda i, j: (0, j)),
                  pl.BlockSpec((te, d), lambda i, j: (j, 0)),
                  pl.BlockSpec((tm, te // nk, nk), lambda i, j: (i, j, 0)),
                  pl.BlockSpec((tm, d), lambda i, j: (i, 0))],
        out_specs=pl.BlockSpec((tm, d), lambda i, j: (i, 0)),
        out_shape=jax.ShapeDtypeStruct((t, d), F32),
        scratch_shapes=[pltpu.VMEM((tm, d), F32)],
        compiler_params=_params("parallel", "arbitrary", vmem=VMEM_LIMIT_FFN),
    )(n2, down_t, up, gd3, h)


def _hybrid_mixer(h, l, bsz, seq, norm_mix_w, w_in, conv_a_w, ssd_conv_w, ssd_conv_b, ssd_dt_bias,
                  ssd_a_log, ssd_d, ssd_norm_w, w_branch_a, w_branch_b, w_branch_c, w_out):
    d = h.shape[1]
    conv_w = conv_a_w.shape[2]
    sb_w = w_branch_b.shape[1]
    ssd_w = w_branch_c.shape[1]
    xbc_w = ssd_conv_w.shape[2]
    heads = ssd_d.shape[1]
    bounds = [0, 3 * conv_w, 3 * sb_w, ssd_w, xbc_w, heads, 3 * d]
    offs = [sum(bounds[:k + 1]) for k in range(len(bounds))]
    w = w_in[l]
    w_a, w_b, w_z, w_xbc, w_dt, w_g = (w[:, offs[k]:offs[k + 1]].astype(BF16) for k in range(6))
    w_dt = jnp.pad(w_dt, ((0, 0), (0, LANES - heads)))

    n = _rmsnorm(h, norm_mix_w[l], BF16)
    u_a = _matmul(n, w_a, BF16)
    u_b = _matmul(n, w_b, BF16)
    u_z = _matmul(n, w_z, BF16)
    u_xbc = _matmul(n, w_xbc, BF16)
    u_dt = _matmul(n, w_dt, F32)
    u_g = _matmul(n, w_g, BF16)

    mix_a = _short_conv(u_a, conv_a_w[l], bsz, seq)
    mix_b = _sb_attention(u_b, bsz, seq)
    xbc_act = _ssd_conv(u_xbc, ssd_conv_w[l], ssd_conv_b[l], bsz, seq)
    mix_c = _ssd(xbc_act, u_z, u_dt, ssd_dt_bias[l], ssd_a_log[l], ssd_d[l], ssd_norm_w[l], bsz, seq)

    m = _matmul(mix_a, w_branch_a[l].astype(BF16), F32, gate=u_g)
    m = _matmul(mix_b, w_branch_b[l].astype(BF16), F32, gate=u_g, gate_col=d, add=m)
    m = _matmul(mix_c, w_branch_c[l].astype(BF16), BF16, gate=u_g, gate_col=2 * d, add=m, tm=512)
    return _matmul(m, w_out[l].astype(BF16), F32, add=h)


def _peer(h, l, norm_ffn_w, peer_w_query, peer_sub_keys, peer_down, peer_up):
    n2 = _rmsnorm(h, norm_ffn_w[l], BF16)
    q = _matmul(n2, peer_w_query[l].astype(BF16), BF16)
    i1t, i2t, gt = _peer_topk(q, peer_sub_keys[l])
    gd3 = _peer_gates(i1t, i2t, gt, peer_sub_keys.shape[3])
    return _peer_ffn(n2, peer_down[l].astype(BF16).T, peer_up[l].astype(BF16), gd3, h)


def kernel(x, norm_mix_w, w_in, conv_a_w, ssd_conv_w, ssd_conv_b, ssd_dt_bias, ssd_a_log, ssd_d,
           ssd_norm_w, w_branch_a, w_branch_b, w_branch_c, w_out, norm_ffn_w, peer_w_query,
           peer_sub_keys, peer_down, peer_up, final_norm_w):
    bsz, seq, d = x.shape
    h = x.reshape(bsz * seq, d)
    for l in range(w_in.shape[0]):
        h = _hybrid_mixer(h, l, bsz, seq, norm_mix_w, w_in, conv_a_w, ssd_conv_w, ssd_conv_b,
                          ssd_dt_bias, ssd_a_log, ssd_d, ssd_norm_w, w_branch_a, w_branch_b,
                          w_branch_c, w_out)
        h = _peer(h, l, norm_ffn_w, peer_w_query, peer_sub_keys, peer_down, peer_up)
    return _rmsnorm(h, final_norm_w, x.dtype).reshape(bsz, seq, d)
```

```python
import functools

import jax
import jax.numpy as jnp
from jax import lax
from jax.experimental import pallas as pl
from jax.experimental.pallas import tpu as pltpu

F32 = jnp.float32
BF16 = jnp.bfloat16
I32 = jnp.int32

EPS = 1e-6
LOG2E = 1.4426950408889634
SB_HEAD_DIM = 128
SSD_HEAD_DIM = 64
SSD_GROUPS = 8
SSD_STATE = 128
SSD_CHUNK = 128
PEER_HEADS = 8
PEER_TOPK = 16

LANES = 128
SUBLANES = 8
VMEM_LIMIT = 48 * 1024 * 1024
VMEM_LIMIT_FFN = 56 * 1024 * 1024


def _params(*sem, vmem=VMEM_LIMIT):
    return pltpu.CompilerParams(dimension_semantics=sem, vmem_limit_bytes=vmem)


def _log2(n):
    assert n & (n - 1) == 0
    return n.bit_length() - 1


def _tile(n, pref):
    if n <= pref:
        return n
    t = pref
    while n % t:
        t -= SUBLANES
    return t


def _rmsnorm_kernel(x_ref, w_ref, o_ref):
    x = x_ref[...].astype(F32)
    ms = jnp.mean(x * x, axis=-1, keepdims=True)
    o_ref[...] = (x * lax.rsqrt(ms + EPS) * w_ref[...]).astype(o_ref.dtype)


def _rmsnorm(x, w, out_dtype):
    t, d = x.shape
    tm = _tile(t, 512)
    return pl.pallas_call(
        _rmsnorm_kernel,
        name="rmsnorm",
        grid=(t // tm,),
        in_specs=[pl.BlockSpec((tm, d), lambda i: (i, 0)),
                  pl.BlockSpec((1, d), lambda i: (0, 0))],
        out_specs=pl.BlockSpec((tm, d), lambda i: (i, 0)),
        out_shape=jax.ShapeDtypeStruct((t, d), out_dtype),
        compiler_params=_params("parallel"),
    )(x, w.reshape(1, d).astype(F32))


def _mm_kernel(*refs, has_gate, has_add):
    x_ref, w_ref = refs[0], refs[1]
    o_ref = refs[-1]
    acc = jnp.dot(x_ref[...], w_ref[...], preferred_element_type=F32)
    k = 2
    if has_gate:
        acc = acc * jax.nn.sigmoid(refs[k][...].astype(F32))
        k += 1
    if has_add:
        acc = acc + refs[k][...].astype(F32)
    o_ref[...] = acc.astype(o_ref.dtype)


def _matmul(x, w, out_dtype, gate=None, gate_col=0, add=None, tm=1024, tn=1024, name="matmul"):
    m, kd = x.shape
    n = w.shape[1]
    tm = _tile(m, tm)
    tn = min(tn, n)
    assert n % tn == 0 and gate_col % tn == 0
    goff = gate_col // tn
    ins = [x, w]
    specs = [pl.BlockSpec((tm, kd), lambda i, j: (i, 0)),
             pl.BlockSpec((kd, tn), lambda i, j: (0, j))]
    if gate is not None:
        ins.append(gate)
        specs.append(pl.BlockSpec((tm, tn), lambda i, j: (i, goff + j)))
    if add is not None:
        ins.append(add)
        specs.append(pl.BlockSpec((tm, tn), lambda i, j: (i, j)))
    return pl.pallas_call(
        functools.partial(_mm_kernel, has_gate=gate is not None, has_add=add is not None),
        name=name,
        grid=(m // tm, n // tn),
        in_specs=specs,
        out_specs=pl.BlockSpec((tm, tn), lambda i, j: (i, j)),
        out_shape=jax.ShapeDtypeStruct((m, n), out_dtype),
        compiler_params=_params("parallel", "parallel"),
    )(*ins)


def _conv_taps(cur, buf_ref, w_ref, first):
    ts = cur.shape[0]
    kw = w_ref.shape[0]

    @pl.when(first)
    def _():
        buf_ref[0:SUBLANES, :] = jnp.zeros((SUBLANES, cur.shape[1]), F32)

    buf_ref[SUBLANES:SUBLANES + ts, :] = cur
    y = cur * w_ref[kw - 1:kw, :]
    for d in range(1, kw):
        y = y + buf_ref[SUBLANES - d:SUBLANES - d + ts, :] * w_ref[kw - 1 - d:kw - d, :]
    buf_ref[0:SUBLANES, :] = cur[ts - SUBLANES:ts, :]
    return y


def _short_conv_kernel(b_ref, c_ref, h_ref, w_ref, o_ref, buf_ref):
    cur = c_ref[...].astype(F32) * h_ref[...].astype(F32)
    y = _conv_taps(cur, buf_ref, w_ref, pl.program_id(2) == 0)
    o_ref[...] = (b_ref[...].astype(F32) * y).astype(o_ref.dtype)


def _short_conv(u_a, conv_w, bsz, seq):
    t = u_a.shape[0]
    width = conv_w.shape[1]
    ts = _tile(seq, 512)
    tc = 512
    nc = width // tc
    ns = seq // ts
    row = lambda b, j, s: b * ns + s
    return pl.pallas_call(
        _short_conv_kernel,
        name="short_conv",
        grid=(bsz, nc, ns),
        in_specs=[pl.BlockSpec((ts, tc), lambda b, j, s: (row(b, j, s), j)),
                  pl.BlockSpec((ts, tc), lambda b, j, s: (row(b, j, s), nc + j)),
                  pl.BlockSpec((ts, tc), lambda b, j, s: (row(b, j, s), 2 * nc + j)),
                  pl.BlockSpec((conv_w.shape[0], tc), lambda b, j, s: (0, j))],
        out_specs=pl.BlockSpec((ts, tc), lambda b, j, s: (row(b, j, s), j)),
        out_shape=jax.ShapeDtypeStruct((t, width), BF16),
        scratch_shapes=[pltpu.VMEM((SUBLANES + ts, tc), F32)],
        compiler_params=_params("parallel", "parallel", "arbitrary"),
    )(u_a, u_a, u_a, conv_w.astype(F32))


def _ssd_conv_kernel(x_ref, w_ref, b_ref, o_ref, buf_ref):
    y = _conv_taps(x_ref[...].astype(F32), buf_ref, w_ref, pl.program_id(2) == 0) + b_ref[...]
    o_ref[...] = (y * jax.nn.sigmoid(y)).astype(o_ref.dtype)


def _ssd_conv(xbc, conv_w, conv_b, bsz, seq):
    t, width = xbc.shape
    ts = _tile(seq, 512)
    tc = 512
    nc = width // tc
    ns = seq // ts
    return pl.pallas_call(
        _ssd_conv_kernel,
        name="ssd_conv",
        grid=(bsz, nc, ns),
        in_specs=[pl.BlockSpec((ts, tc), lambda b, j, s: (b * ns + s, j)),
                  pl.BlockSpec((conv_w.shape[0], tc), lambda b, j, s: (0, j)),
                  pl.BlockSpec((1, tc), lambda b, j, s: (0, j))],
        out_specs=pl.BlockSpec((ts, tc), lambda b, j, s: (b * ns + s, j)),
        out_shape=jax.ShapeDtypeStruct((t, width), BF16),
        scratch_shapes=[pltpu.VMEM((SUBLANES + ts, tc), F32)],
        compiler_params=_params("parallel", "parallel", "arbitrary"),
    )(xbc, conv_w.astype(F32), conv_b.reshape(1, width).astype(F32))


def _split2(x):
    hi = x.astype(BF16)
    lo = (x - hi.astype(F32)).astype(BF16)
    return hi, lo


def _split3(x):
    hi = x.astype(BF16)
    r = x - hi.astype(F32)
    mid = r.astype(BF16)
    lo = (r - mid.astype(F32)).astype(BF16)
    return hi, mid, lo


def _sb_attn_kernel(q_ref, k_ref, v_ref, o_ref, *, bq, bk, scale):
    i = pl.program_id(2)
    dh = SB_HEAD_DIM
    nh = q_ref.shape[1] // dh
    nkb = bq // bk
    row = lax.broadcasted_iota(I32, (bq, bk), 0)
    col = lax.broadcasted_iota(I32, (bq, bk), 1)
    krow = lax.broadcasted_iota(I32, (bk, bk), 0)
    kcol = lax.broadcasted_iota(I32, (bk, bk), 1)
    later_mat = jnp.where(krow > kcol, 1.0, 0.0).astype(BF16)
    qs = [(q_ref[:, h * dh:(h + 1) * dh].astype(F32) * (scale * LOG2E)).astype(BF16) for h in range(nh)]

    def tile(h, start, run, mask):
        ks = k_ref[pl.ds(start, bk), h * dh:(h + 1) * dh]
        vs = v_ref[pl.ds(start, bk), h * dh:(h + 1) * dh]
        z = lax.dot_general(qs[h], ks, (((1,), (1,)), ((), ())), preferred_element_type=F32)
        neg = jnp.minimum(z, 0.0)
        pos = z - neg
        ls = neg - jnp.log(1.0 + jnp.exp2(neg - pos)) * LOG2E
        lnb = ls - z
        if mask is not None:
            lnb = jnp.where(mask, lnb, 0.0)
        lnb = lnb.astype(BF16)
        later = jnp.dot(lnb, later_mat, preferred_element_type=F32)
        a = jnp.exp2(ls + later + run)
        if mask is not None:
            a = jnp.where(mask, a, 0.0)
        o = jnp.dot(a.astype(BF16), vs, preferred_element_type=F32)
        return o, run + later[:, 0:1] + lnb[:, 0:1].astype(F32)

    state = [(jnp.zeros((bq, dh), F32), jnp.zeros((bq, 1), F32)) for _ in range(nh)]
    for d in range(nkb):
        off = (nkb - 1 - d) * bk
        start = pl.multiple_of(i * bq + off, bk)
        for h in range(nh):
            o, run = tile(h, start, state[h][1], col + off < row)
            state[h] = (state[h][0] + o, run)

    def body(jj, carry):
        out = list(carry)
        for d in range(nkb):
            start = pl.multiple_of(((i - jj) * nkb - 1 - d) * bk, bk)
            for h in range(nh):
                o, run = tile(h, start, out[h][1], None)
                out[h] = (out[h][0] + o, run)
        return tuple(out)

    res = lax.fori_loop(0, i, body, tuple(state))
    o_ref[...] = jnp.concatenate([r[0] for r in res], axis=1).astype(o_ref.dtype)


SB_HEADS_PER_STEP = 2
SB_Q_BLOCK = 512
SB_K_BLOCK = 256


def _sb_attention(u_b, bsz, seq):
    t, w3 = u_b.shape
    width = w3 // 3
    hw = SB_HEADS_PER_STEP * SB_HEAD_DIM
    groups = width // hw
    bq = _tile(seq, SB_Q_BLOCK)
    bk = min(SB_K_BLOCK, bq)
    assert bq % bk == 0
    nq = seq // bq
    return pl.pallas_call(
        functools.partial(_sb_attn_kernel, bq=bq, bk=bk, scale=SB_HEAD_DIM ** -0.5),
        grid=(bsz, groups, nq),
        in_specs=[pl.BlockSpec((bq, hw), lambda b, h, i: (b * nq + i, h)),
                  pl.BlockSpec((seq, hw), lambda b, h, i: (b, groups + h)),
                  pl.BlockSpec((seq, hw), lambda b, h, i: (b, 2 * groups + h))],
        out_specs=pl.BlockSpec((bq, hw), lambda b, h, i: (b * nq + i, h)),
        out_shape=jax.ShapeDtypeStruct((t, width), BF16),
        name="sb_attention",
        compiler_params=_params("parallel", "parallel", "arbitrary"),
    )(u_b, u_b, u_b)


def _ssd_kernel(xs_ref, bm_ref, cm_ref, z_ref, dt_ref, dtb_ref, alog_ref, dsk_ref, nw_ref,
                o_ref, state_ref, acst_ref, *, hpg, hd):
    g = pl.program_id(1)
    c = pl.program_id(2)
    lc = xs_ref.shape[0]
    gw = hpg * hd

    @pl.when(c == 0)
    def _():
        state_ref[...] = jnp.zeros(state_ref.shape, F32)

    dt = jax.nn.softplus(dt_ref[...] + dtb_ref[...])
    da = dt * (-jnp.exp(alog_ref[...]))
    row = lax.broadcasted_iota(I32, (lc, lc), 0)
    col = lax.broadcasted_iota(I32, (lc, lc), 1)
    causal = row >= col
    tri = jnp.where(causal, 1.0, 0.0).astype(BF16)
    a_cs = sum(jnp.dot(tri, part, preferred_element_type=F32) for part in _split3(da))
    acst_ref[...] = a_cs.T
    exp_acs = jnp.exp(a_cs)
    dte = jnp.exp(a_cs[lc - 1:lc, :] - a_cs)

    hrow = lax.broadcasted_iota(I32, (LANES, gw), 0)
    hcol = lax.broadcasted_iota(I32, (LANES, gw), 1)
    spread = jnp.where(hrow == g * hpg + (hcol >> _log2(hd)), 1.0, 0.0).astype(BF16)
    hrow2 = lax.broadcasted_iota(I32, (LANES, hpg * lc), 0)
    hcol2 = lax.broadcasted_iota(I32, (LANES, hpg * lc), 1)
    spread_l = jnp.where(hrow2 == g * hpg + (hcol2 >> _log2(lc)), 1.0, 0.0).astype(BF16)

    def expand(x, parts, mat):
        pieces = (_split3(x) if parts == 3 else _split2(x) if parts == 2 else (x.astype(BF16),))
        return sum(jnp.dot(p, mat, preferred_element_type=F32) for p in pieces)

    dt_x = expand(dt, 1, spread)
    dte_x = expand(dte, 1, spread)
    eacs_x = expand(exp_acs, 2, spread)
    dsk_x = expand(jnp.broadcast_to(dsk_ref[...], (SUBLANES, LANES)), 3, spread)[0:1, :]
    acs_col = expand(a_cs, 2, spread_l)

    xs = xs_ref[...].astype(F32)
    bm = bm_ref[...]
    cm = cm_ref[...]
    xdt = xs * dt_x
    xdt_b = xdt.astype(BF16)
    cb = lax.dot_general(cm, bm, (((1,), (1,)), ((), ())), preferred_element_type=F32)

    y_parts = []
    for r in range(hpg):
        a_l = acs_col[:, r * lc:(r + 1) * lc]
        a_s = acst_ref[pl.ds(g * hpg + r, 1), :]
        seg = jnp.where(causal, jnp.exp(jnp.where(causal, a_l - a_s, 0.0)), 0.0)
        m = (cb * seg).astype(BF16)
        y_parts.append(jnp.dot(m, xdt_b[:, r * hd:(r + 1) * hd], preferred_element_type=F32))
    y_diag = jnp.concatenate(y_parts, axis=1)

    prev = state_ref[...]
    y_off = jnp.dot(cm, prev.astype(BF16), preferred_element_type=F32) * eacs_x
    y = y_diag + y_off + xs * dsk_x

    bm_t = bm.astype(F32).T.astype(BF16)
    new_state = jnp.dot(bm_t, (xdt * dte_x).astype(BF16), preferred_element_type=F32)
    state_ref[...] = prev * eacs_x[lc - 1:lc, :] + new_state

    zf = z_ref[...].astype(F32)
    yg = y * (zf * jax.nn.sigmoid(zf))
    ms = jnp.mean(yg * yg, axis=-1, keepdims=True)
    o_ref[...] = (yg * lax.rsqrt(ms + EPS) * nw_ref[...]).astype(o_ref.dtype)


def _pad_lanes(v):
    return jnp.zeros((1, LANES), F32).at[0, :v.shape[0]].set(v.astype(F32))


def _ssd(xbc_act, z, dt_raw, dt_bias, a_log, d_skip, norm_w, bsz, seq):
    t, width = z.shape
    heads = width // SSD_HEAD_DIM
    hpg = heads // SSD_GROUPS
    gw = hpg * SSD_HEAD_DIM
    lc = SSD_CHUNK
    nch = seq // lc
    nxg = width // gw
    nbs = width // SSD_STATE
    assert gw * SSD_GROUPS == width and SSD_STATE == LANES and heads <= LANES
    row = lambda b, g, c: b * nch + c
    return pl.pallas_call(
        functools.partial(_ssd_kernel, hpg=hpg, hd=SSD_HEAD_DIM),
        name="ssd",
        grid=(bsz, SSD_GROUPS, nch),
        in_specs=[pl.BlockSpec((lc, gw), lambda b, g, c: (row(b, g, c), g)),
                  pl.BlockSpec((lc, SSD_STATE), lambda b, g, c: (row(b, g, c), nbs + g)),
                  pl.BlockSpec((lc, SSD_STATE), lambda b, g, c: (row(b, g, c), nbs + SSD_GROUPS + g)),
                  pl.BlockSpec((lc, gw), lambda b, g, c: (row(b, g, c), g)),
                  pl.BlockSpec((lc, LANES), lambda b, g, c: (row(b, g, c), 0)),
                  pl.BlockSpec((1, LANES), lambda b, g, c: (0, 0)),
                  pl.BlockSpec((1, LANES), lambda b, g, c: (0, 0)),
                  pl.BlockSpec((1, LANES), lambda b, g, c: (0, 0)),
                  pl.BlockSpec((1, gw), lambda b, g, c: (0, g))],
        out_specs=pl.BlockSpec((lc, gw), lambda b, g, c: (row(b, g, c), g)),
        out_shape=jax.ShapeDtypeStruct((t, width), BF16),
        scratch_shapes=[pltpu.VMEM((SSD_STATE, gw), F32), pltpu.VMEM((LANES, lc), F32)],
        compiler_params=_params("parallel", "parallel", "arbitrary"),
    )(xbc_act, xbc_act, xbc_act, z, dt_raw, _pad_lanes(dt_bias), _pad_lanes(a_log),
      _pad_lanes(d_skip), norm_w.reshape(1, width).astype(F32))


def _topk_rows(s, k):
    n = s.shape[0]
    iota = lax.broadcasted_iota(I32, s.shape, 0)
    vals, idxs = [], []
    for _ in range(k):
        m = jnp.max(s, axis=0, keepdims=True)
        i = jnp.min(jnp.where(s == m, iota, n), axis=0, keepdims=True)
        vals.append(m)
        idxs.append(i)
        s = jnp.where(iota == i, -jnp.inf, s)
    return jnp.concatenate(vals, axis=0), jnp.concatenate(idxs, axis=0)


def _peer_topk_kernel(q_ref, keys_ref, i1_ref, i2_ref, g_ref, *, topk):
    half = q_ref.shape[1] // 2
    sub_v, sub_i = [], []
    for c in range(2):
        s = lax.dot_general(keys_ref[c], q_ref[:, c * half:(c + 1) * half],
                            (((1,), (1,)), ((), ())), preferred_element_type=F32)
        v, i = _topk_rows(s, topk)
        sub_v.append(v)
        sub_i.append(i)
    n_wide = topk // 2
    pieces, starts = [], []
    for a in range(n_wide):
        nb = min(topk, -(-(topk // (a + 1)) // SUBLANES) * SUBLANES)
        starts.append(sum(p.shape[0] for p in pieces))
        pieces.append(sub_v[0][a:a + 1, :] + sub_v[1][0:nb, :])
    tail_start = sum(p.shape[0] for p in pieces)
    pieces.append(sub_v[0][n_wide:topk, :] + sub_v[1][0:1, :])
    top_s, pos = _topk_rows(jnp.concatenate(pieces, axis=0), topk)
    pa = jnp.zeros(pos.shape, I32)
    pb = pos
    for a in range(1, n_wide):
        ge = pos >= starts[a]
        pa = jnp.where(ge, a, pa)
        pb = jnp.where(ge, pos - starts[a], pb)
    ge = pos >= tail_start
    pa = jnp.where(ge, pos + (n_wide - tail_start), pa)
    pb = jnp.where(ge, 0, pb)
    i1 = jnp.zeros(pos.shape, I32)
    i2 = jnp.zeros(pos.shape, I32)
    for a in range(topk):
        i1 = jnp.where(pa == a, sub_i[0][a:a + 1, :], i1)
        i2 = jnp.where(pb == a, sub_i[1][a:a + 1, :], i2)
    e = jnp.exp(top_s - top_s[0:1, :])
    i1_ref[...] = i1
    i2_ref[...] = i2
    g_ref[...] = e / jnp.sum(e, axis=0, keepdims=True)


def _peer_topk(q, sub_keys):
    t, qw = q.shape
    heads, _, nkeys, dk2 = sub_keys.shape
    tt = _tile(t, 256)
    hk = heads * PEER_TOPK
    out = jax.ShapeDtypeStruct((hk, t), I32)
    spec = pl.BlockSpec((PEER_TOPK, tt), lambda i, h: (h, i))
    return pl.pallas_call(
        functools.partial(_peer_topk_kernel, topk=PEER_TOPK),
        name="peer_topk",
        grid=(t // tt, heads),
        in_specs=[pl.BlockSpec((tt, 2 * dk2), lambda i, h: (i, h)),
                  pl.BlockSpec((None, 2, nkeys, dk2), lambda i, h: (h, 0, 0, 0))],
        out_specs=[spec, spec, spec],
        out_shape=[out, out, jax.ShapeDtypeStruct((hk, t), F32)],
        compiler_params=_params("parallel", "parallel"),
    )(q, sub_keys.astype(BF16))


def _peer_gates_kernel(i1_ref, i2_ref, g_ref, o_ref, i1s_ref, i2s_ref, gs_ref):
    nk = o_ref.shape[1]
    i1s_ref[...] = i1_ref[...].astype(F32).T
    i2s_ref[...] = i2_ref[...].astype(F32).T
    gs_ref[...] = g_ref[...].T
    key = lax.broadcasted_iota(I32, (nk, i1_ref.shape[0]), 0).astype(F32)

    def body(t, _):
        r1 = i1s_ref[pl.ds(t, 1), :]
        r2 = i2s_ref[pl.ds(t, 1), :]
        gg = gs_ref[pl.ds(t, 1), :]
        ghi = gg.astype(BF16).astype(F32)
        glo = gg - ghi
        eq1 = key == r1
        a_mat = jnp.concatenate([jnp.where(eq1, ghi, 0.0), jnp.where(eq1, glo, 0.0)],
                                axis=1).astype(BF16)
        b_one = jnp.where(key == r2, 1.0, 0.0)
        b_mat = jnp.concatenate([b_one, b_one], axis=1).astype(BF16)
        o_ref[t] = lax.dot_general(a_mat, b_mat, (((1,), (1,)), ((), ())),
                                   preferred_element_type=F32)
        return 0

    lax.fori_loop(0, o_ref.shape[0], body, 0, unroll=SUBLANES)


def _peer_gates(i1t, i2t, gt, nkeys):
    hk, t = i1t.shape
    tt = _tile(t, 128)
    spec = pl.BlockSpec((hk, tt), lambda i: (0, i))
    return pl.pallas_call(
        _peer_gates_kernel,
        name="peer_gates",
        grid=(t // tt,),
        in_specs=[spec, spec, spec],
        out_specs=pl.BlockSpec((tt, nkeys, nkeys), lambda i: (i, 0, 0)),
        out_shape=jax.ShapeDtypeStruct((t, nkeys, nkeys), F32),
        scratch_shapes=[pltpu.VMEM((tt, hk), F32)] * 3,
        compiler_params=_params("parallel"),
    )(i1t, i2t, gt)


def _peer_ffn_kernel(x_ref, dn_ref, up_ref, gd_ref, h_ref, o_ref, acc_ref):
    j = pl.program_id(1)

    @pl.when(j == 0)
    def _():
        acc_ref[...] = jnp.zeros(acc_ref.shape, F32)

    s = jnp.dot(x_ref[...], dn_ref[...], preferred_element_type=F32)
    nk = gd_ref.shape[2]
    pieces = []
    for c in range(gd_ref.shape[1]):
        sc = s[:, c * nk:(c + 1) * nk]
        act = 0.5 * sc * (1.0 + lax.erf(sc * (2.0 ** -0.5)))
        pieces.append((act * gd_ref[:, c, :]).astype(BF16))
    a = jnp.concatenate(pieces, axis=1)
    acc_ref[...] += jnp.dot(a, up_ref[...], preferred_element_type=F32)

    @pl.when(j == pl.num_programs(1) - 1)
    def _():
        o_ref[...] = h_ref[...] + acc_ref[...]


def _peer_ffn(n2, down_t, up, gd3, h):
    t, d = n2.shape
    ne = up.shape[0]
    nk = gd3.shape[2]
    tm = _tile(t, 512)
    te = SUBLANES * nk
    return pl.pallas_call(
        _peer_ffn_kernel,
        name="peer_ffn",
        grid=(t // tm, ne // te),
        in_specs=[pl.BlockSpec((tm, d), lambda i, j: (i, 0)),
                  pl.BlockSpec((d, te), lambda i, j: (0, j)),
                  pl.BlockSpec((te, d), lambda i, j: (j, 0)),
                  pl.BlockSpec((tm, te // nk, nk), lambda i, j: (i, j, 0)),
                  pl.BlockSpec((tm, d), lambda i, j: (i, 0))],
        out_specs=pl.BlockSpec((tm, d), lambda i, j: (i, 0)),
        out_shape=jax.ShapeDtypeStruct((t, d), F32),
        scratch_shapes=[pltpu.VMEM((tm, d), F32)],
        compiler_params=_params("parallel", "arbitrary", vmem=VMEM_LIMIT_FFN),
    )(n2, down_t, up, gd3, h)


def _hybrid_mixer(h, l, bsz, seq, norm_mix_w, w_in, conv_a_w, ssd_conv_w, ssd_conv_b, ssd_dt_bias,
                  ssd_a_log, ssd_d, ssd_norm_w, w_branch_a, w_branch_b, w_branch_c, w_out):
    d = h.shape[1]
    conv_w = conv_a_w.shape[2]
    sb_w = w_branch_b.shape[1]
    ssd_w = w_branch_c.shape[1]
    xbc_w = ssd_conv_w.shape[2]
    heads = ssd_d.shape[1]
    bounds = [0, 3 * conv_w, 3 * sb_w, ssd_w, xbc_w, heads, 3 * d]
    offs = [sum(bounds[:k + 1]) for k in range(len(bounds))]
    w = w_in[l]
    w_a, w_b, w_z, w_xbc, w_dt, w_g = (w[:, offs[k]:offs[k + 1]].astype(BF16) for k in range(6))
    w_dt = jnp.pad(w_dt, ((0, 0), (0, LANES - heads)))

    n = _rmsnorm(h, norm_mix_w[l], BF16)
    u_a = _matmul(n, w_a, BF16)
    u_b = _matmul(n, w_b, BF16)
    u_z = _matmul(n, w_z, BF16)
    u_xbc = _matmul(n, w_xbc, BF16)
    u_dt = _matmul(n, w_dt, F32)
    u_g = _matmul(n, w_g, BF16)

    mix_a = _short_conv(u_a, conv_a_w[l], bsz, seq)
    mix_b = _sb_attention(u_b, bsz, seq)
    xbc_act = _ssd_conv(u_xbc, ssd_conv_w[l], ssd_conv_b[l], bsz, seq)
    mix_c = _ssd(xbc_act, u_z, u_dt, ssd_dt_bias[l], ssd_a_log[l], ssd_d[l], ssd_norm_w[l], bsz, seq)

    m = _matmul(mix_a, w_branch_a[l].astype(BF16), F32, gate=u_g)
    m = _matmul(mix_b, w_branch_b[l].astype(BF16), F32, gate=u_g, gate_col=d, add=m)
    m = _matmul(mix_c, w_branch_c[l].astype(BF16), BF16, gate=u_g, gate_col=2 * d, add=m, tm=512)
    return _matmul(m, w_out[l].astype(BF16), F32, add=h)


def _peer(h, l, norm_ffn_w, peer_w_query, peer_sub_keys, peer_down, peer_up):
    n2 = _rmsnorm(h, norm_ffn_w[l], BF16)
    q = _matmul(n2, peer_w_query[l].astype(BF16), BF16)
    i1t, i2t, gt = _peer_topk(q, peer_sub_keys[l])
    gd3 = _peer_gates(i1t, i2t, gt, peer_sub_keys.shape[3])
    return _peer_ffn(n2, peer_down[l].astype(BF16).T, peer_up[l].astype(BF16), gd3, h)


def kernel(x, norm_mix_w, w_in, conv_a_w, ssd_conv_w, ssd_conv_b, ssd_dt_bias, ssd_a_log, ssd_d,
           ssd_norm_w, w_branch_a, w_branch_b, w_branch_c, w_out, norm_ffn_w, peer_w_query,
           peer_sub_keys, peer_down, peer_up, final_norm_w):
    bsz, seq, d = x.shape
    h = x.reshape(bsz * seq, d)
    for l in range(w_in.shape[0]):
        h = _hybrid_mixer(h, l, bsz, seq, norm_mix_w, w_in, conv_a_w, ssd_conv_w, ssd_conv_b,
                          ssd_dt_bias, ssd_a_log, ssd_d, ssd_norm_w, w_branch_a, w_branch_b,
                          w_branch_c, w_out)
        h = _peer(h, l, norm_ffn_w, peer_w_query, peer_sub_keys, peer_down, peer_up)
    return _rmsnorm(h, final_norm_w, x.dtype).reshape(bsz, seq, d)
```

```python
import functools

import jax
import jax.numpy as jnp
from jax import lax
from jax.experimental import pallas as pl
from jax.experimental.pallas import tpu as pltpu

F32 = jnp.float32
BF16 = jnp.bfloat16
I32 = jnp.int32

EPS = 1e-6
LOG2E = 1.4426950408889634
SB_HEAD_DIM = 128
SSD_HEAD_DIM = 64
SSD_GROUPS = 8
SSD_STATE = 128
SSD_CHUNK = 128
SSD_CHUNKS_PER_STEP = 4
GATES_UNROLL = 32
PEER_HEADS = 8
PEER_TOPK = 16

LANES = 128
SUBLANES = 8
VMEM_LIMIT = 48 * 1024 * 1024
VMEM_LIMIT_FFN = 56 * 1024 * 1024


def _params(*sem, vmem=VMEM_LIMIT):
    return pltpu.CompilerParams(dimension_semantics=sem, vmem_limit_bytes=vmem)


def _log2(n):
    assert n & (n - 1) == 0
    return n.bit_length() - 1


def _tile(n, pref):
    if n <= pref:
        return n
    t = pref
    while n % t:
        t -= SUBLANES
    return t


def _rmsnorm_kernel(x_ref, w_ref, o_ref):
    x = x_ref[...].astype(F32)
    ms = jnp.mean(x * x, axis=-1, keepdims=True)
    o_ref[...] = (x * lax.rsqrt(ms + EPS) * w_ref[...]).astype(o_ref.dtype)


def _add_rmsnorm_kernel(x_ref, a_ref, w_ref, *out_refs):
    h = x_ref[...] + a_ref[...]
    if len(out_refs) == 2:
        out_refs[0][...] = h
    ms = jnp.mean(h * h, axis=-1, keepdims=True)
    out_refs[-1][...] = (h * lax.rsqrt(ms + EPS) * w_ref[...]).astype(out_refs[-1].dtype)


def _rmsnorm(x, w, out_dtype, add=None, keep_sum=True):
    t, d = x.shape
    tm = _tile(t, 512)
    row = pl.BlockSpec((tm, d), lambda i: (i, 0))
    wspec = pl.BlockSpec((1, d), lambda i: (0, 0))
    normed = jax.ShapeDtypeStruct((t, d), out_dtype)
    w2 = w.reshape(1, d).astype(F32)
    if add is None:
        return pl.pallas_call(
            _rmsnorm_kernel,
            name="rmsnorm",
            grid=(t // tm,),
            in_specs=[row, wspec],
            out_specs=row,
            out_shape=normed,
            compiler_params=_params("parallel"),
        )(x, w2)
    return pl.pallas_call(
        _add_rmsnorm_kernel,
        name="add_rmsnorm",
        grid=(t // tm,),
        in_specs=[row, row, wspec],
        out_specs=[row, row] if keep_sum else row,
        out_shape=[jax.ShapeDtypeStruct((t, d), F32), normed] if keep_sum else normed,
        compiler_params=_params("parallel"),
    )(x, add, w2)


def _mm_kernel(*refs, has_gate, has_add):
    x_ref, w_ref = refs[0], refs[1]
    o_ref = refs[-1]
    acc = jnp.dot(x_ref[...], w_ref[...], preferred_element_type=F32)
    k = 2
    if has_gate:
        acc = acc * jax.nn.sigmoid(refs[k][...].astype(F32))
        k += 1
    if has_add:
        acc = acc + refs[k][...].astype(F32)
    o_ref[...] = acc.astype(o_ref.dtype)


def _matmul(x, w, out_dtype, gate=None, gate_col=0, add=None, tm=1024, tn=1024, name="matmul"):
    m, kd = x.shape
    n = w.shape[1]
    tm = _tile(m, tm)
    tn = min(tn, n)
    assert n % tn == 0 and gate_col % tn == 0
    goff = gate_col // tn
    ins = [x, w]
    specs = [pl.BlockSpec((tm, kd), lambda i, j: (i, 0)),
             pl.BlockSpec((kd, tn), lambda i, j: (0, j))]
    if gate is not None:
        ins.append(gate)
        specs.append(pl.BlockSpec((tm, tn), lambda i, j: (i, goff + j)))
    if add is not None:
        ins.append(add)
        specs.append(pl.BlockSpec((tm, tn), lambda i, j: (i, j)))
    return pl.pallas_call(
        functools.partial(_mm_kernel, has_gate=gate is not None, has_add=add is not None),
        name=name,
        grid=(m // tm, n // tn),
        in_specs=specs,
        out_specs=pl.BlockSpec((tm, tn), lambda i, j: (i, j)),
        out_shape=jax.ShapeDtypeStruct((m, n), out_dtype),
        compiler_params=_params("parallel", "parallel"),
    )(*ins)


def _conv_taps(cur, buf_ref, w_ref, first):
    ts = cur.shape[0]
    kw = w_ref.shape[0]

    @pl.when(first)
    def _():
        buf_ref[0:SUBLANES, :] = jnp.zeros((SUBLANES, cur.shape[1]), F32)

    buf_ref[SUBLANES:SUBLANES + ts, :] = cur
    y = cur * w_ref[kw - 1:kw, :]
    for d in range(1, kw):
        y = y + buf_ref[SUBLANES - d:SUBLANES - d + ts, :] * w_ref[kw - 1 - d:kw - d, :]
    buf_ref[0:SUBLANES, :] = cur[ts - SUBLANES:ts, :]
    return y


def _short_conv_kernel(b_ref, c_ref, h_ref, w_ref, o_ref, buf_ref):
    cur = c_ref[...].astype(F32) * h_ref[...].astype(F32)
    y = _conv_taps(cur, buf_ref, w_ref, pl.program_id(2) == 0)
    o_ref[...] = (b_ref[...].astype(F32) * y).astype(o_ref.dtype)


def _short_conv(u_a, conv_w, bsz, seq):
    t = u_a.shape[0]
    width = conv_w.shape[1]
    ts = _tile(seq, 512)
    tc = 512
    nc = width // tc
    ns = seq // ts
    row = lambda b, j, s: b * ns + s
    return pl.pallas_call(
        _short_conv_kernel,
        name="short_conv",
        grid=(bsz, nc, ns),
        in_specs=[pl.BlockSpec((ts, tc), lambda b, j, s: (row(b, j, s), j)),
                  pl.BlockSpec((ts, tc), lambda b, j, s: (row(b, j, s), nc + j)),
                  pl.BlockSpec((ts, tc), lambda b, j, s: (row(b, j, s), 2 * nc + j)),
                  pl.BlockSpec((conv_w.shape[0], tc), lambda b, j, s: (0, j))],
        out_specs=pl.BlockSpec((ts, tc), lambda b, j, s: (row(b, j, s), j)),
        out_shape=jax.ShapeDtypeStruct((t, width), BF16),
        scratch_shapes=[pltpu.VMEM((SUBLANES + ts, tc), F32)],
        compiler_params=_params("parallel", "parallel", "arbitrary"),
    )(u_a, u_a, u_a, conv_w.astype(F32))


def _ssd_conv_kernel(x_ref, w_ref, b_ref, o_ref, buf_ref):
    y = _conv_taps(x_ref[...].astype(F32), buf_ref, w_ref, pl.program_id(2) == 0) + b_ref[...]
    o_ref[...] = (y * jax.nn.sigmoid(y)).astype(o_ref.dtype)


def _ssd_conv(xbc, conv_w, conv_b, bsz, seq):
    t, width = xbc.shape
    ts = _tile(seq, 512)
    tc = 512
    nc = width // tc
    ns = seq // ts
    return pl.pallas_call(
        _ssd_conv_kernel,
        name="ssd_conv",
        grid=(bsz, nc, ns),
        in_specs=[pl.BlockSpec((ts, tc), lambda b, j, s: (b * ns + s, j)),
                  pl.BlockSpec((conv_w.shape[0], tc), lambda b, j, s: (0, j)),
                  pl.BlockSpec((1, tc), lambda b, j, s: (0, j))],
        out_specs=pl.BlockSpec((ts, tc), lambda b, j, s: (b * ns + s, j)),
        out_shape=jax.ShapeDtypeStruct((t, width), BF16),
        scratch_shapes=[pltpu.VMEM((SUBLANES + ts, tc), F32)],
        compiler_params=_params("parallel", "parallel", "arbitrary"),
    )(xbc, conv_w.astype(F32), conv_b.reshape(1, width).astype(F32))


def _split2(x):
    hi = x.astype(BF16)
    lo = (x - hi.astype(F32)).astype(BF16)
    return hi, lo


def _split3(x):
    hi = x.astype(BF16)
    r = x - hi.astype(F32)
    mid = r.astype(BF16)
    lo = (r - mid.astype(F32)).astype(BF16)
    return hi, mid, lo


def _sb_attn_kernel(q_ref, k_ref, v_ref, o_ref, *, bq, bk, scale):
    i = pl.program_id(2)
    dh = SB_HEAD_DIM
    nh = q_ref.shape[1] // dh
    nkb = bq // bk
    row = lax.broadcasted_iota(I32, (bq, bk), 0)
    col = lax.broadcasted_iota(I32, (bq, bk), 1)
    krow = lax.broadcasted_iota(I32, (bk, bk), 0)
    kcol = lax.broadcasted_iota(I32, (bk, bk), 1)
    later_mat = jnp.where(krow > kcol, 1.0, 0.0).astype(BF16)
    qs = [(q_ref[:, h * dh:(h + 1) * dh].astype(F32) * (scale * LOG2E)).astype(BF16) for h in range(nh)]

    def tile(h, start, run, mask):
        ks = k_ref[pl.ds(start, bk), h * dh:(h + 1) * dh]
        vs = v_ref[pl.ds(start, bk), h * dh:(h + 1) * dh]
        z = lax.dot_general(qs[h], ks, (((1,), (1,)), ((), ())), preferred_element_type=F32)
        neg = jnp.minimum(z, 0.0)
        pos = z - neg
        ls = neg - jnp.log(1.0 + jnp.exp2(neg - pos)) * LOG2E
        lnb = ls - z
        if mask is not None:
            lnb = jnp.where(mask, lnb, 0.0)
        lnb = lnb.astype(BF16)
        later = jnp.dot(lnb, later_mat, preferred_element_type=F32)
        a = jnp.exp2(ls + later + run)
        if mask is not None:
            a = jnp.where(mask, a, 0.0)
        o = jnp.dot(a.astype(BF16), vs, preferred_element_type=F32)
        return o, run + later[:, 0:1] + lnb[:, 0:1].astype(F32)

    state = [(jnp.zeros((bq, dh), F32), jnp.zeros((bq, 1), F32)) for _ in range(nh)]
    for d in range(nkb):
        off = (nkb - 1 - d) * bk
        start = pl.multiple_of(i * bq + off, bk)
        for h in range(nh):
            o, run = tile(h, start, state[h][1], col + off < row)
            state[h] = (state[h][0] + o, run)

    def body(jj, carry):
        out = list(carry)
        for d in range(nkb):
            start = pl.multiple_of(((i - jj) * nkb - 1 - d) * bk, bk)
            for h in range(nh):
                o, run = tile(h, start, out[h][1], None)
                out[h] = (out[h][0] + o, run)
        return tuple(out)

    res = lax.fori_loop(0, i, body, tuple(state))
    o_ref[...] = jnp.concatenate([r[0] for r in res], axis=1).astype(o_ref.dtype)


SB_HEADS_PER_STEP = 4
SB_Q_BLOCK = 512
SB_K_BLOCK = 256


def _sb_attention(u_b, bsz, seq):
    t, w3 = u_b.shape
    width = w3 // 3
    hw = SB_HEADS_PER_STEP * SB_HEAD_DIM
    groups = width // hw
    bq = _tile(seq, SB_Q_BLOCK)
    bk = min(SB_K_BLOCK, bq)
    assert bq % bk == 0
    nq = seq // bq
    return pl.pallas_call(
        functools.partial(_sb_attn_kernel, bq=bq, bk=bk, scale=SB_HEAD_DIM ** -0.5),
        grid=(bsz, groups, nq),
        in_specs=[pl.BlockSpec((bq, hw), lambda b, h, i: (b * nq + i, h)),
                  pl.BlockSpec((seq, hw), lambda b, h, i: (b, groups + h)),
                  pl.BlockSpec((seq, hw), lambda b, h, i: (b, 2 * groups + h))],
        out_specs=pl.BlockSpec((bq, hw), lambda b, h, i: (b * nq + i, h)),
        out_shape=jax.ShapeDtypeStruct((t, width), BF16),
        name="sb_attention",
        compiler_params=_params("parallel", "parallel", "arbitrary"),
    )(u_b, u_b, u_b)


def _ssd_kernel(xs_ref, bm_ref, cm_ref, z_ref, dt_ref, dtb_ref, alog_ref, dsk_ref, nw_ref,
                o_ref, state_ref, acst_ref, *, hpg, hd, lc):
    g = pl.program_id(1)
    gw = hpg * hd
    nsub = xs_ref.shape[0] // lc

    @pl.when(pl.program_id(2) == 0)
    def _():
        state_ref[...] = jnp.zeros(state_ref.shape, F32)

    row = lax.broadcasted_iota(I32, (lc, lc), 0)
    col = lax.broadcasted_iota(I32, (lc, lc), 1)
    causal = row >= col
    tri = jnp.where(causal, 1.0, 0.0).astype(BF16)
    hrow = lax.broadcasted_iota(I32, (LANES, gw), 0)
    hcol = lax.broadcasted_iota(I32, (LANES, gw), 1)
    spread = jnp.where(hrow == g * hpg + (hcol >> _log2(hd)), 1.0, 0.0).astype(BF16)
    hrow2 = lax.broadcasted_iota(I32, (LANES, hpg * lc), 0)
    hcol2 = lax.broadcasted_iota(I32, (LANES, hpg * lc), 1)
    spread_l = jnp.where(hrow2 == g * hpg + (hcol2 >> _log2(lc)), 1.0, 0.0).astype(BF16)

    def expand(x, parts, mat):
        pieces = (_split3(x) if parts == 3 else _split2(x) if parts == 2 else (x.astype(BF16),))
        return sum(jnp.dot(p, mat, preferred_element_type=F32) for p in pieces)

    neg_a = -jnp.exp(alog_ref[...])
    dsk_x = expand(jnp.broadcast_to(dsk_ref[...], (SUBLANES, LANES)), 3, spread)[0:1, :]

    def chunk(sc, prev):
        rows = slice(sc * lc, (sc + 1) * lc)
        dt = jax.nn.softplus(dt_ref[rows, :] + dtb_ref[...])
        a_cs = sum(jnp.dot(tri, part, preferred_element_type=F32) for part in _split3(dt * neg_a))
        acst_ref[sc] = a_cs.T
        exp_acs = jnp.exp(a_cs)
        dte = jnp.exp(a_cs[lc - 1:lc, :] - a_cs)

        dt_x = expand(dt, 1, spread)
        dte_x = expand(dte, 1, spread)
        eacs_x = expand(exp_acs, 2, spread)
        acs_col = expand(a_cs, 2, spread_l)

        xs = xs_ref[rows, :].astype(F32)
        bm = bm_ref[rows, :]
        cm = cm_ref[rows, :]
        xdt = xs * dt_x
        xdt_b = xdt.astype(BF16)
        cb = lax.dot_general(cm, bm, (((1,), (1,)), ((), ())), preferred_element_type=F32)

        y_parts = []
        for r in range(hpg):
            a_l = acs_col[:, r * lc:(r + 1) * lc]
            a_s = acst_ref[sc, pl.ds(g * hpg + r, 1), :]
            seg = jnp.where(causal, jnp.exp(jnp.where(causal, a_l - a_s, 0.0)), 0.0)
            m = (cb * seg).astype(BF16)
            y_parts.append(jnp.dot(m, xdt_b[:, r * hd:(r + 1) * hd], preferred_element_type=F32))
        y_diag = jnp.concatenate(y_parts, axis=1)

        y_off = jnp.dot(cm, prev.astype(BF16), preferred_element_type=F32) * eacs_x
        y = y_diag + y_off + xs * dsk_x

        bm_t = bm.astype(F32).T.astype(BF16)
        new_state = jnp.dot(bm_t, (xdt * dte_x).astype(BF16), preferred_element_type=F32)

        zf = z_ref[rows, :].astype(F32)
        yg = y * (zf * jax.nn.sigmoid(zf))
        ms = jnp.mean(yg * yg, axis=-1, keepdims=True)
        o_ref[rows, :] = (yg * lax.rsqrt(ms + EPS) * nw_ref[...]).astype(o_ref.dtype)
        return prev * eacs_x[lc - 1:lc, :] + new_state

    state = state_ref[...]
    for sc in range(nsub):
        state = chunk(sc, state)
    state_ref[...] = state


def _pad_lanes(v):
    return jnp.zeros((1, LANES), F32).at[0, :v.shape[0]].set(v.astype(F32))


def _ssd(xbc_act, z, dt_raw, dt_bias, a_log, d_skip, norm_w, bsz, seq):
    t, width = z.shape
    heads = width // SSD_HEAD_DIM
    hpg = heads // SSD_GROUPS
    gw = hpg * SSD_HEAD_DIM
    nsub = SSD_CHUNKS_PER_STEP if seq % (SSD_CHUNKS_PER_STEP * SSD_CHUNK) == 0 else 1
    lc = nsub * SSD_CHUNK
    nch = seq // lc
    nbs = width // SSD_STATE
    assert gw * SSD_GROUPS == width and SSD_STATE == LANES and heads <= LANES
    row = lambda b, g, c: b * nch + c
    return pl.pallas_call(
        functools.partial(_ssd_kernel, hpg=hpg, hd=SSD_HEAD_DIM, lc=SSD_CHUNK),
        name="ssd",
        grid=(bsz, SSD_GROUPS, nch),
        in_specs=[pl.BlockSpec((lc, gw), lambda b, g, c: (row(b, g, c), g)),
                  pl.BlockSpec((lc, SSD_STATE), lambda b, g, c: (row(b, g, c), nbs + g)),
                  pl.BlockSpec((lc, SSD_STATE), lambda b, g, c: (row(b, g, c), nbs + SSD_GROUPS + g)),
                  pl.BlockSpec((lc, gw), lambda b, g, c: (row(b, g, c), g)),
                  pl.BlockSpec((lc, LANES), lambda b, g, c: (row(b, g, c), 0)),
                  pl.BlockSpec((1, LANES), lambda b, g, c: (0, 0)),
                  pl.BlockSpec((1, LANES), lambda b, g, c: (0, 0)),
                  pl.BlockSpec((1, LANES), lambda b, g, c: (0, 0)),
                  pl.BlockSpec((1, gw), lambda b, g, c: (0, g))],
        out_specs=pl.BlockSpec((lc, gw), lambda b, g, c: (row(b, g, c), g)),
        out_shape=jax.ShapeDtypeStruct((t, width), BF16),
        scratch_shapes=[pltpu.VMEM((SSD_STATE, gw), F32), pltpu.VMEM((nsub, LANES, SSD_CHUNK), F32)],
        compiler_params=_params("parallel", "parallel", "arbitrary"),
    )(xbc_act, xbc_act, xbc_act, z, dt_raw, _pad_lanes(dt_bias), _pad_lanes(a_log),
      _pad_lanes(d_skip), norm_w.reshape(1, width).astype(F32))


def _topk_rows(s, k):
    n = s.shape[0]
    iota = lax.broadcasted_iota(I32, s.shape, 0)
    vals, idxs = [], []
    for _ in range(k):
        m = jnp.max(s, axis=0, keepdims=True)
        i = jnp.min(jnp.where(s == m, iota, n), axis=0, keepdims=True)
        vals.append(m)
        idxs.append(i)
        s = jnp.where(iota == i, -jnp.inf, s)
    return jnp.concatenate(vals, axis=0), jnp.concatenate(idxs, axis=0)


def _peer_topk_kernel(q_ref, keys_ref, i1_ref, i2_ref, g_ref, *, topk):
    half = q_ref.shape[1] // 2
    sub_v, sub_i = [], []
    for c in range(2):
        s = lax.dot_general(keys_ref[c], q_ref[:, c * half:(c + 1) * half],
                            (((1,), (1,)), ((), ())), preferred_element_type=F32)
        v, i = _topk_rows(s, topk)
        sub_v.append(v)
        sub_i.append(i)
    n_wide = topk // 2
    pieces, starts = [], []
    for a in range(n_wide):
        nb = min(topk, -(-(topk // (a + 1)) // SUBLANES) * SUBLANES)
        starts.append(sum(p.shape[0] for p in pieces))
        pieces.append(sub_v[0][a:a + 1, :] + sub_v[1][0:nb, :])
    tail_start = sum(p.shape[0] for p in pieces)
    pieces.append(sub_v[0][n_wide:topk, :] + sub_v[1][0:1, :])
    top_s, pos = _topk_rows(jnp.concatenate(pieces, axis=0), topk)
    pa = jnp.zeros(pos.shape, I32)
    pb = pos
    for a in range(1, n_wide):
        ge = pos >= starts[a]
        pa = jnp.where(ge, a, pa)
        pb = jnp.where(ge, pos - starts[a], pb)
    ge = pos >= tail_start
    pa = jnp.where(ge, pos + (n_wide - tail_start), pa)
    pb = jnp.where(ge, 0, pb)
    i1 = jnp.zeros(pos.shape, I32)
    i2 = jnp.zeros(pos.shape, I32)
    for a in range(topk):
        i1 = jnp.where(pa == a, sub_i[0][a:a + 1, :], i1)
        i2 = jnp.where(pb == a, sub_i[1][a:a + 1, :], i2)
    e = jnp.exp(top_s - top_s[0:1, :])
    i1_ref[...] = i1
    i2_ref[...] = i2
    g_ref[...] = e / jnp.sum(e, axis=0, keepdims=True)


def _peer_topk(q, sub_keys):
    t, qw = q.shape
    heads, _, nkeys, dk2 = sub_keys.shape
    tt = _tile(t, 256)
    hk = heads * PEER_TOPK
    out = jax.ShapeDtypeStruct((hk, t), I32)
    spec = pl.BlockSpec((PEER_TOPK, tt), lambda i, h: (h, i))
    return pl.pallas_call(
        functools.partial(_peer_topk_kernel, topk=PEER_TOPK),
        name="peer_topk",
        grid=(t // tt, heads),
        in_specs=[pl.BlockSpec((tt, 2 * dk2), lambda i, h: (i, h)),
                  pl.BlockSpec((None, 2, nkeys, dk2), lambda i, h: (h, 0, 0, 0))],
        out_specs=[spec, spec, spec],
        out_shape=[out, out, jax.ShapeDtypeStruct((hk, t), F32)],
        compiler_params=_params("parallel", "parallel"),
    )(q, sub_keys.astype(BF16))


def _peer_gates_kernel(i1_ref, i2_ref, g_ref, o_ref, i1s_ref, i2s_ref, gs_ref):
    nk = o_ref.shape[2]
    i1s_ref[...] = i1_ref[...].astype(F32).T
    i2s_ref[...] = i2_ref[...].astype(F32).T
    gs_ref[...] = g_ref[...].T
    key = lax.broadcasted_iota(I32, (nk, i1_ref.shape[0]), 0).astype(F32)

    def body(t, _):
        r1 = i1s_ref[pl.ds(t, 1), :]
        r2 = i2s_ref[pl.ds(t, 1), :]
        gg = gs_ref[pl.ds(t, 1), :]
        a_mat = jnp.where(key == r1, gg, 0.0).astype(BF16)
        b_mat = jnp.where(key == r2, 1.0, 0.0).astype(BF16)
        gmap = lax.dot_general(a_mat, b_mat, (((1,), (1,)), ((), ())), preferred_element_type=F32)
        o_ref[:, pl.ds(pl.multiple_of(t * SUBLANES, SUBLANES), SUBLANES), :] = gmap.reshape(
            nk // SUBLANES, SUBLANES, nk)
        return 0

    lax.fori_loop(0, i1s_ref.shape[0], body, 0, unroll=GATES_UNROLL)


def _peer_gates(i1t, i2t, gt, nkeys):
    hk, t = i1t.shape
    tt = _tile(t, 128)
    spec = pl.BlockSpec((hk, tt), lambda i: (0, i))
    return pl.pallas_call(
        _peer_gates_kernel,
        name="peer_gates",
        grid=(t // tt,),
        in_specs=[spec, spec, spec],
        out_specs=pl.BlockSpec((nkeys // SUBLANES, tt * SUBLANES, nkeys), lambda i: (0, i, 0)),
        out_shape=jax.ShapeDtypeStruct((nkeys // SUBLANES, t * SUBLANES, nkeys), F32),
        scratch_shapes=[pltpu.VMEM((tt, hk), F32)] * 3,
        compiler_params=_params("parallel"),
    )(i1t, i2t, gt)


def _peer_ffn_kernel(x_ref, dn_ref, up_ref, gd_ref, o_ref, s_ref):
    j = pl.program_id(1)
    tm = x_ref.shape[0]
    nk = gd_ref.shape[1]
    per = SUBLANES // 2
    half = o_ref.shape[1] // 2

    @pl.when(j == 0)
    def _():
        o_ref[...] = jnp.zeros(o_ref.shape, F32)
        s_ref[1] = jnp.zeros(s_ref.shape[1:], F32)

    def step(prev):
        pieces = []
        for c in range(per):
            sc = s_ref[prev, :, c * nk:(c + 1) * nk]
            act = 0.5 * sc * (1.0 + lax.erf(sc * (2.0 ** -0.5)))
            gate = gd_ref[pl.ds(prev * per + c, tm, stride=SUBLANES), :]
            pieces.append((act * gate).astype(BF16))
        a = jnp.concatenate(pieces, axis=1)
        s_new = jnp.dot(x_ref[...], dn_ref[...], preferred_element_type=F32)
        for n in range(2):
            cols = slice(n * half, (n + 1) * half)
            o_ref[:, cols] += jnp.dot(a, up_ref[:, cols], preferred_element_type=F32)
        s_ref[1 - prev] = s_new

    for prev in range(2):
        pl.when(lax.rem(j + 1, 2) == prev)(functools.partial(step, prev))


def _peer_ffn(n2, down_t, up, gd3):
    t, d = n2.shape
    ne = up.shape[0]
    nk = gd3.shape[2]
    tm = _tile(t, 1024)
    te = SUBLANES * nk // 2
    nj = ne // te
    assert nj % 2 == 0
    last = lambda j: jnp.maximum(j - 1, 0)
    return pl.pallas_call(
        _peer_ffn_kernel,
        name="peer_ffn",
        grid=(t // tm, nj + 1),
        in_specs=[pl.BlockSpec((tm, d), lambda i, j: (i, 0)),
                  pl.BlockSpec((d, te), lambda i, j: (0, jnp.minimum(j, nj - 1))),
                  pl.BlockSpec((te, d), lambda i, j: (last(j), 0)),
                  pl.BlockSpec((None, tm * SUBLANES, nk), lambda i, j: (last(j) // 2, i, 0))],
        out_specs=pl.BlockSpec((tm, d), lambda i, j: (i, 0)),
        out_shape=jax.ShapeDtypeStruct((t, d), F32),
        scratch_shapes=[pltpu.VMEM((2, tm, te), F32)],
        compiler_params=_params("parallel", "arbitrary", vmem=VMEM_LIMIT_FFN),
    )(n2, down_t, up, gd3)


def _hybrid_mixer(h, n, l, bsz, seq, w_in, conv_a_w, ssd_conv_w, ssd_conv_b, ssd_dt_bias,
                  ssd_a_log, ssd_d, ssd_norm_w, w_branch_a, w_branch_b, w_branch_c, w_out):
    d = h.shape[1]
    conv_w = conv_a_w.shape[2]
    sb_w = w_branch_b.shape[1]
    ssd_w = w_branch_c.shape[1]
    xbc_w = ssd_conv_w.shape[2]
    heads = ssd_d.shape[1]
    bounds = [0, 3 * conv_w, 3 * sb_w, ssd_w, xbc_w, heads, 3 * d]
    offs = [sum(bounds[:k + 1]) for k in range(len(bounds))]
    w = w_in[l]
    w_a, w_b, w_z, w_xbc, w_dt, w_g = (w[:, offs[k]:offs[k + 1]].astype(BF16) for k in range(6))
    w_dt = jnp.pad(w_dt, ((0, 0), (0, LANES - heads)))

    u_a = _matmul(n, w_a, BF16)
    u_b = _matmul(n, w_b, BF16)
    u_z = _matmul(n, w_z, BF16)
    u_xbc = _matmul(n, w_xbc, BF16)
    u_dt = _matmul(n, w_dt, F32)
    u_g = _matmul(n, w_g, BF16)

    mix_a = _short_conv(u_a, conv_a_w[l], bsz, seq)
    mix_b = _sb_attention(u_b, bsz, seq)
    xbc_act = _ssd_conv(u_xbc, ssd_conv_w[l], ssd_conv_b[l], bsz, seq)
    mix_c = _ssd(xbc_act, u_z, u_dt, ssd_dt_bias[l], ssd_a_log[l], ssd_d[l], ssd_norm_w[l], bsz, seq)

    m = _matmul(mix_a, w_branch_a[l].astype(BF16), F32, gate=u_g)
    m = _matmul(mix_b, w_branch_b[l].astype(BF16), F32, gate=u_g, gate_col=d, add=m)
    m = _matmul(mix_c, w_branch_c[l].astype(BF16), BF16, gate=u_g, gate_col=2 * d, add=m, tm=512)
    return _matmul(m, w_out[l].astype(BF16), F32, add=h)


def _peer(h, l, norm_ffn_w, peer_w_query, peer_sub_keys, peer_down, peer_up):
    n2 = _rmsnorm(h, norm_ffn_w[l], BF16)
    q = _matmul(n2, peer_w_query[l].astype(BF16), BF16)
    i1t, i2t, gt = _peer_topk(q, peer_sub_keys[l])
    gd3 = _peer_gates(i1t, i2t, gt, peer_sub_keys.shape[3])
    return _peer_ffn(n2, peer_down[l].astype(BF16).T, peer_up[l].astype(BF16), gd3)


def kernel(x, norm_mix_w, w_in, conv_a_w, ssd_conv_w, ssd_conv_b, ssd_dt_bias, ssd_a_log, ssd_d,
           ssd_norm_w, w_branch_a, w_branch_b, w_branch_c, w_out, norm_ffn_w, peer_w_query,
           peer_sub_keys, peer_down, peer_up, final_norm_w):
    bsz, seq, d = x.shape
    h = x.reshape(bsz * seq, d)
    ffn = None
    for l in range(w_in.shape[0]):
        if ffn is None:
            n = _rmsnorm(h, norm_mix_w[l], BF16)
        else:
            h, n = _rmsnorm(h, norm_mix_w[l], BF16, add=ffn)
        h = _hybrid_mixer(h, n, l, bsz, seq, w_in, conv_a_w, ssd_conv_w, ssd_conv_b,
                          ssd_dt_bias, ssd_a_log, ssd_d, ssd_norm_w, w_branch_a, w_branch_b,
                          w_branch_c, w_out)
        ffn = _peer(h, l, norm_ffn_w, peer_w_query, peer_sub_keys, peer_down, peer_up)
    return _rmsnorm(h, final_norm_w, x.dtype, add=ffn, keep_sum=False).reshape(bsz, seq, d)
```

```python
import functools

import jax
import jax.numpy as jnp
from jax import lax
from jax.experimental import pallas as pl
from jax.experimental.pallas import tpu as pltpu

F32 = jnp.float32
BF16 = jnp.bfloat16
I32 = jnp.int32

EPS = 1e-6
LOG2E = 1.4426950408889634
SB_HEAD_DIM = 128
SSD_HEAD_DIM = 64
SSD_GROUPS = 8
SSD_STATE = 128
SSD_CHUNK = 128
SSD_CHUNKS_PER_STEP = 4
GATES_UNROLL = 64
PEER_HEADS = 8
PEER_TOPK = 16

LANES = 128
SUBLANES = 8
VMEM_LIMIT = 48 * 1024 * 1024
VMEM_LIMIT_FFN = 56 * 1024 * 1024


def _params(*sem, vmem=VMEM_LIMIT):
    return pltpu.CompilerParams(dimension_semantics=sem, vmem_limit_bytes=vmem)


def _log2(n):
    assert n & (n - 1) == 0
    return n.bit_length() - 1


def _tile(n, pref):
    if n <= pref:
        return n
    t = pref
    while n % t:
        t -= SUBLANES
    return t


def _rmsnorm_kernel(x_ref, w_ref, o_ref):
    x = x_ref[...].astype(F32)
    ms = jnp.mean(x * x, axis=-1, keepdims=True)
    o_ref[...] = (x * lax.rsqrt(ms + EPS) * w_ref[...]).astype(o_ref.dtype)


def _add_rmsnorm_kernel(x_ref, a_ref, w_ref, *out_refs):
    h = x_ref[...] + a_ref[...]
    if len(out_refs) == 2:
        out_refs[0][...] = h
    ms = jnp.mean(h * h, axis=-1, keepdims=True)
    out_refs[-1][...] = (h * lax.rsqrt(ms + EPS) * w_ref[...]).astype(out_refs[-1].dtype)


def _rmsnorm(x, w, out_dtype, add=None, keep_sum=True):
    t, d = x.shape
    tm = _tile(t, 512)
    row = pl.BlockSpec((tm, d), lambda i: (i, 0))
    wspec = pl.BlockSpec((1, d), lambda i: (0, 0))
    normed = jax.ShapeDtypeStruct((t, d), out_dtype)
    w2 = w.reshape(1, d).astype(F32)
    if add is None:
        return pl.pallas_call(
            _rmsnorm_kernel,
            name="rmsnorm",
            grid=(t // tm,),
            in_specs=[row, wspec],
            out_specs=row,
            out_shape=normed,
            compiler_params=_params("parallel"),
        )(x, w2)
    return pl.pallas_call(
        _add_rmsnorm_kernel,
        name="add_rmsnorm",
        grid=(t // tm,),
        in_specs=[row, row, wspec],
        out_specs=[row, row] if keep_sum else row,
        out_shape=[jax.ShapeDtypeStruct((t, d), F32), normed] if keep_sum else normed,
        compiler_params=_params("parallel"),
    )(x, add, w2)


def _mm_kernel(*refs, has_gate, has_add):
    x_ref, w_ref = refs[0], refs[1]
    o_ref = refs[-1]
    acc = jnp.dot(x_ref[...], w_ref[...], preferred_element_type=F32)
    k = 2
    if has_gate:
        acc = acc * jax.nn.sigmoid(refs[k][...].astype(F32))
        k += 1
    if has_add:
        acc = acc + refs[k][...].astype(F32)
    o_ref[...] = acc.astype(o_ref.dtype)


def _matmul(x, w, out_dtype, gate=None, gate_col=0, add=None, tm=1024, tn=1024, name="matmul"):
    m, kd = x.shape
    n = w.shape[1]
    tm = _tile(m, tm)
    tn = min(tn, n)
    assert n % tn == 0 and gate_col % tn == 0
    goff = gate_col // tn
    ins = [x, w]
    specs = [pl.BlockSpec((tm, kd), lambda i, j: (i, 0)),
             pl.BlockSpec((kd, tn), lambda i, j: (0, j))]
    if gate is not None:
        ins.append(gate)
        specs.append(pl.BlockSpec((tm, tn), lambda i, j: (i, goff + j)))
    if add is not None:
        ins.append(add)
        specs.append(pl.BlockSpec((tm, tn), lambda i, j: (i, j)))
    return pl.pallas_call(
        functools.partial(_mm_kernel, has_gate=gate is not None, has_add=add is not None),
        name=name,
        grid=(m // tm, n // tn),
        in_specs=specs,
        out_specs=pl.BlockSpec((tm, tn), lambda i, j: (i, j)),
        out_shape=jax.ShapeDtypeStruct((m, n), out_dtype),
        compiler_params=_params("parallel", "parallel"),
    )(*ins)


def _conv_taps(cur, buf_ref, w_ref, first):
    ts = cur.shape[0]
    kw = w_ref.shape[0]

    @pl.when(first)
    def _():
        buf_ref[0:SUBLANES, :] = jnp.zeros((SUBLANES, cur.shape[1]), F32)

    buf_ref[SUBLANES:SUBLANES + ts, :] = cur
    y = cur * w_ref[kw - 1:kw, :]
    for d in range(1, kw):
        y = y + buf_ref[SUBLANES - d:SUBLANES - d + ts, :] * w_ref[kw - 1 - d:kw - d, :]
    buf_ref[0:SUBLANES, :] = cur[ts - SUBLANES:ts, :]
    return y


def _short_conv_kernel(b_ref, c_ref, h_ref, w_ref, o_ref, buf_ref):
    cur = c_ref[...].astype(F32) * h_ref[...].astype(F32)
    y = _conv_taps(cur, buf_ref, w_ref, pl.program_id(2) == 0)
    o_ref[...] = (b_ref[...].astype(F32) * y).astype(o_ref.dtype)


def _short_conv(u_a, conv_w, bsz, seq):
    t = u_a.shape[0]
    width = conv_w.shape[1]
    ts = _tile(seq, 512)
    tc = 512
    nc = width // tc
    ns = seq // ts
    row = lambda b, j, s: b * ns + s
    return pl.pallas_call(
        _short_conv_kernel,
        name="short_conv",
        grid=(bsz, nc, ns),
        in_specs=[pl.BlockSpec((ts, tc), lambda b, j, s: (row(b, j, s), j)),
                  pl.BlockSpec((ts, tc), lambda b, j, s: (row(b, j, s), nc + j)),
                  pl.BlockSpec((ts, tc), lambda b, j, s: (row(b, j, s), 2 * nc + j)),
                  pl.BlockSpec((conv_w.shape[0], tc), lambda b, j, s: (0, j))],
        out_specs=pl.BlockSpec((ts, tc), lambda b, j, s: (row(b, j, s), j)),
        out_shape=jax.ShapeDtypeStruct((t, width), BF16),
        scratch_shapes=[pltpu.VMEM((SUBLANES + ts, tc), F32)],
        compiler_params=_params("parallel", "parallel", "arbitrary"),
    )(u_a, u_a, u_a, conv_w.astype(F32))


def _ssd_conv_kernel(x_ref, w_ref, b_ref, o_ref, buf_ref):
    y = _conv_taps(x_ref[...].astype(F32), buf_ref, w_ref, pl.program_id(2) == 0) + b_ref[...]
    o_ref[...] = (y * jax.nn.sigmoid(y)).astype(o_ref.dtype)


def _ssd_conv(xbc, conv_w, conv_b, bsz, seq):
    t, width = xbc.shape
    ts = _tile(seq, 512)
    tc = 512
    nc = width // tc
    ns = seq // ts
    return pl.pallas_call(
        _ssd_conv_kernel,
        name="ssd_conv",
        grid=(bsz, nc, ns),
        in_specs=[pl.BlockSpec((ts, tc), lambda b, j, s: (b * ns + s, j)),
                  pl.BlockSpec((conv_w.shape[0], tc), lambda b, j, s: (0, j)),
                  pl.BlockSpec((1, tc), lambda b, j, s: (0, j))],
        out_specs=pl.BlockSpec((ts, tc), lambda b, j, s: (b * ns + s, j)),
        out_shape=jax.ShapeDtypeStruct((t, width), BF16),
        scratch_shapes=[pltpu.VMEM((SUBLANES + ts, tc), F32)],
        compiler_params=_params("parallel", "parallel", "arbitrary"),
    )(xbc, conv_w.astype(F32), conv_b.reshape(1, width).astype(F32))


def _split2(x):
    hi = x.astype(BF16)
    lo = (x - hi.astype(F32)).astype(BF16)
    return hi, lo


def _split3(x):
    hi = x.astype(BF16)
    r = x - hi.astype(F32)
    mid = r.astype(BF16)
    lo = (r - mid.astype(F32)).astype(BF16)
    return hi, mid, lo


def _sb_attn_kernel(q_ref, k_ref, v_ref, o_ref, *, bq, bk, scale):
    i = pl.program_id(2)
    dh = SB_HEAD_DIM
    nh = q_ref.shape[1] // dh
    nkb = bq // bk
    row = lax.broadcasted_iota(I32, (bq, bk), 0)
    col = lax.broadcasted_iota(I32, (bq, bk), 1)
    krow = lax.broadcasted_iota(I32, (bk, bk), 0)
    kcol = lax.broadcasted_iota(I32, (bk, bk), 1)
    later_mat = jnp.where(krow > kcol, 1.0, 0.0).astype(BF16)
    qs = [(q_ref[:, h * dh:(h + 1) * dh].astype(F32) * (scale * LOG2E)).astype(BF16) for h in range(nh)]

    def tile(h, start, run, mask):
        ks = k_ref[pl.ds(start, bk), h * dh:(h + 1) * dh]
        vs = v_ref[pl.ds(start, bk), h * dh:(h + 1) * dh]
        z = lax.dot_general(qs[h], ks, (((1,), (1,)), ((), ())), preferred_element_type=F32)
        z = z.astype(BF16)
        neg = jnp.minimum(z, 0.0)
        pos = z - neg
        ls = neg - jnp.log(1.0 + jnp.exp2(neg - pos)) * LOG2E
        lnb = ls - z
        if mask is not None:
            lnb = jnp.where(mask, lnb.astype(F32), 0.0).astype(BF16)
        later = jnp.dot(lnb, later_mat, preferred_element_type=F32)
        a = jnp.exp2(ls.astype(F32) + later + run)
        if mask is not None:
            a = jnp.where(mask, a, 0.0)
        o = jnp.dot(a.astype(BF16), vs, preferred_element_type=F32)
        return o, run + later[:, 0:1] + lnb[:, 0:1].astype(F32)

    state = [(jnp.zeros((bq, dh), F32), jnp.zeros((bq, 1), F32)) for _ in range(nh)]
    for d in range(nkb):
        off = (nkb - 1 - d) * bk
        start = pl.multiple_of(i * bq + off, bk)
        for h in range(nh):
            o, run = tile(h, start, state[h][1], col + off < row)
            state[h] = (state[h][0] + o, run)

    def body(jj, carry):
        out = list(carry)
        for d in range(nkb):
            start = pl.multiple_of(((i - jj) * nkb - 1 - d) * bk, bk)
            for h in range(nh):
                o, run = tile(h, start, out[h][1], None)
                out[h] = (out[h][0] + o, run)
        return tuple(out)

    res = lax.fori_loop(0, i, body, tuple(state))
    o_ref[...] = jnp.concatenate([r[0] for r in res], axis=1).astype(o_ref.dtype)


SB_HEADS_PER_STEP = 2
SB_Q_BLOCK = 1024
SB_K_BLOCK = 256


def _sb_attention(u_b, bsz, seq):
    t, w3 = u_b.shape
    width = w3 // 3
    hw = SB_HEADS_PER_STEP * SB_HEAD_DIM
    groups = width // hw
    bq = _tile(seq, SB_Q_BLOCK)
    bk = min(SB_K_BLOCK, bq)
    assert bq % bk == 0
    nq = seq // bq
    return pl.pallas_call(
        functools.partial(_sb_attn_kernel, bq=bq, bk=bk, scale=SB_HEAD_DIM ** -0.5),
        grid=(bsz, groups, nq),
        in_specs=[pl.BlockSpec((bq, hw), lambda b, h, i: (b * nq + i, h)),
                  pl.BlockSpec((seq, hw), lambda b, h, i: (b, groups + h)),
                  pl.BlockSpec((seq, hw), lambda b, h, i: (b, 2 * groups + h))],
        out_specs=pl.BlockSpec((bq, hw), lambda b, h, i: (b * nq + i, h)),
        out_shape=jax.ShapeDtypeStruct((t, width), BF16),
        name="sb_attention",
        compiler_params=_params("parallel", "parallel", "arbitrary"),
    )(u_b, u_b, u_b)


def _ssd_kernel(xs_ref, bm_ref, cm_ref, z_ref, dt_ref, dtb_ref, alog_ref, dsk_ref, nw_ref,
                o_ref, state_ref, acst_ref, *, hpg, hd, lc):
    g = pl.program_id(1)
    gw = hpg * hd
    nsub = xs_ref.shape[0] // lc

    @pl.when(pl.program_id(2) == 0)
    def _():
        state_ref[...] = jnp.zeros(state_ref.shape, F32)

    row = lax.broadcasted_iota(I32, (lc, lc), 0)
    col = lax.broadcasted_iota(I32, (lc, lc), 1)
    causal = row >= col
    tri = jnp.where(causal, 1.0, 0.0).astype(BF16)
    hrow = lax.broadcasted_iota(I32, (LANES, gw), 0)
    hcol = lax.broadcasted_iota(I32, (LANES, gw), 1)
    spread = jnp.where(hrow == g * hpg + (hcol >> _log2(hd)), 1.0, 0.0).astype(BF16)
    hrow2 = lax.broadcasted_iota(I32, (LANES, hpg * lc), 0)
    hcol2 = lax.broadcasted_iota(I32, (LANES, hpg * lc), 1)
    spread_l = jnp.where(hrow2 == g * hpg + (hcol2 >> _log2(lc)), 1.0, 0.0).astype(BF16)

    def expand(x, parts, mat):
        pieces = (_split3(x) if parts == 3 else _split2(x) if parts == 2 else (x.astype(BF16),))
        return sum(jnp.dot(p, mat, preferred_element_type=F32) for p in pieces)

    neg_a = -jnp.exp(alog_ref[...])
    dsk_x = expand(jnp.broadcast_to(dsk_ref[...], (SUBLANES, LANES)), 3, spread)[0:1, :]

    def chunk(sc, prev):
        rows = slice(sc * lc, (sc + 1) * lc)
        dt = jax.nn.softplus(dt_ref[rows, :] + dtb_ref[...])
        a_cs = sum(jnp.dot(tri, part, preferred_element_type=F32) for part in _split3(dt * neg_a))
        acst_ref[sc] = a_cs.T
        exp_acs = jnp.exp(a_cs)
        dte = jnp.exp(a_cs[lc - 1:lc, :] - a_cs)

        dt_x = expand(dt, 1, spread)
        dte_x = expand(dte, 1, spread)
        eacs_x = expand(exp_acs, 2, spread)
        acs_col = expand(a_cs, 2, spread_l)

        xs = xs_ref[rows, :].astype(F32)
        bm = bm_ref[rows, :]
        cm = cm_ref[rows, :]
        xdt = xs * dt_x
        xdt_b = xdt.astype(BF16)
        cb = lax.dot_general(cm, bm, (((1,), (1,)), ((), ())), preferred_element_type=F32)

        y_parts = []
        for r in range(hpg):
            a_l = acs_col[:, r * lc:(r + 1) * lc]
            a_s = acst_ref[sc, pl.ds(g * hpg + r, 1), :]
            seg = jnp.where(causal, jnp.exp(jnp.where(causal, a_l - a_s, 0.0)), 0.0)
            m = (cb * seg).astype(BF16)
            y_parts.append(jnp.dot(m, xdt_b[:, r * hd:(r + 1) * hd], preferred_element_type=F32))
        y_diag = jnp.concatenate(y_parts, axis=1)

        y_off = jnp.dot(cm, prev.astype(BF16), preferred_element_type=F32) * eacs_x
        y = y_diag + y_off + xs * dsk_x

        bm_t = bm.astype(F32).T.astype(BF16)
        new_state = jnp.dot(bm_t, (xdt * dte_x).astype(BF16), preferred_element_type=F32)

        zf = z_ref[rows, :].astype(F32)
        yg = y * (zf * jax.nn.sigmoid(zf))
        ms = jnp.mean(yg * yg, axis=-1, keepdims=True)
        o_ref[rows, :] = (yg * lax.rsqrt(ms + EPS) * nw_ref[...]).astype(o_ref.dtype)
        return prev * eacs_x[lc - 1:lc, :] + new_state

    state = state_ref[...]
    for sc in range(nsub):
        state = chunk(sc, state)
    state_ref[...] = state


def _pad_lanes(v):
    return jnp.zeros((1, LANES), F32).at[0, :v.shape[0]].set(v.astype(F32))


def _ssd(xbc_act, z, dt_raw, dt_bias, a_log, d_skip, norm_w, bsz, seq):
    t, width = z.shape
    heads = width // SSD_HEAD_DIM
    hpg = heads // SSD_GROUPS
    gw = hpg * SSD_HEAD_DIM
    nsub = SSD_CHUNKS_PER_STEP if seq % (SSD_CHUNKS_PER_STEP * SSD_CHUNK) == 0 else 1
    lc = nsub * SSD_CHUNK
    nch = seq // lc
    nbs = width // SSD_STATE
    assert gw * SSD_GROUPS == width and SSD_STATE == LANES and heads <= LANES
    row = lambda b, g, c: b * nch + c
    return pl.pallas_call(
        functools.partial(_ssd_kernel, hpg=hpg, hd=SSD_HEAD_DIM, lc=SSD_CHUNK),
        name="ssd",
        grid=(bsz, SSD_GROUPS, nch),
        in_specs=[pl.BlockSpec((lc, gw), lambda b, g, c: (row(b, g, c), g)),
                  pl.BlockSpec((lc, SSD_STATE), lambda b, g, c: (row(b, g, c), nbs + g)),
                  pl.BlockSpec((lc, SSD_STATE), lambda b, g, c: (row(b, g, c), nbs + SSD_GROUPS + g)),
                  pl.BlockSpec((lc, gw), lambda b, g, c: (row(b, g, c), g)),
                  pl.BlockSpec((lc, LANES), lambda b, g, c: (row(b, g, c), 0)),
                  pl.BlockSpec((1, LANES), lambda b, g, c: (0, 0)),
                  pl.BlockSpec((1, LANES), lambda b, g, c: (0, 0)),
                  pl.BlockSpec((1, LANES), lambda b, g, c: (0, 0)),
                  pl.BlockSpec((1, gw), lambda b, g, c: (0, g))],
        out_specs=pl.BlockSpec((lc, gw), lambda b, g, c: (row(b, g, c), g)),
        out_shape=jax.ShapeDtypeStruct((t, width), BF16),
        scratch_shapes=[pltpu.VMEM((SSD_STATE, gw), F32), pltpu.VMEM((nsub, LANES, SSD_CHUNK), F32)],
        compiler_params=_params("parallel", "parallel", "arbitrary"),
    )(xbc_act, xbc_act, xbc_act, z, dt_raw, _pad_lanes(dt_bias), _pad_lanes(a_log),
      _pad_lanes(d_skip), norm_w.reshape(1, width).astype(F32))


def _topk_rows(s, k):
    n = s.shape[0]
    iota = lax.broadcasted_iota(I32, s.shape, 0)
    vals, idxs = [], []
    for _ in range(k):
        m = jnp.max(s, axis=0, keepdims=True)
        i = jnp.min(jnp.where(s == m, iota, n), axis=0, keepdims=True)
        vals.append(m)
        idxs.append(i)
        s = jnp.where(iota == i, -jnp.inf, s)
    return jnp.concatenate(vals, axis=0), jnp.concatenate(idxs, axis=0)


def _peer_topk_kernel(q_ref, keys_ref, i1_ref, i2_ref, g_ref, *, topk):
    half = q_ref.shape[1] // 2
    sub_v, sub_i = [], []
    for c in range(2):
        s = lax.dot_general(keys_ref[c], q_ref[:, c * half:(c + 1) * half],
                            (((1,), (1,)), ((), ())), preferred_element_type=F32)
        v, i = _topk_rows(s, topk)
        sub_v.append(v)
        sub_i.append(i)
    n_wide = topk // 2
    pieces, starts = [], []
    for a in range(n_wide):
        nb = min(topk, -(-(topk // (a + 1)) // SUBLANES) * SUBLANES)
        starts.append(sum(p.shape[0] for p in pieces))
        pieces.append(sub_v[0][a:a + 1, :] + sub_v[1][0:nb, :])
    tail_start = sum(p.shape[0] for p in pieces)
    pieces.append(sub_v[0][n_wide:topk, :] + sub_v[1][0:1, :])
    top_s, pos = _topk_rows(jnp.concatenate(pieces, axis=0), topk)
    pa = jnp.zeros(pos.shape, I32)
    pb = pos
    for a in range(1, n_wide):
        ge = pos >= starts[a]
        pa = jnp.where(ge, a, pa)
        pb = jnp.where(ge, pos - starts[a], pb)
    ge = pos >= tail_start
    pa = jnp.where(ge, pos + (n_wide - tail_start), pa)
    pb = jnp.where(ge, 0, pb)
    i1 = jnp.zeros(pos.shape, I32)
    i2 = jnp.zeros(pos.shape, I32)
    for a in range(topk):
        i1 = jnp.where(pa == a, sub_i[0][a:a + 1, :], i1)
        i2 = jnp.where(pb == a, sub_i[1][a:a + 1, :], i2)
    e = jnp.exp(top_s - top_s[0:1, :])
    i1_ref[...] = i1
    i2_ref[...] = i2
    g_ref[...] = e / jnp.sum(e, axis=0, keepdims=True)


def _peer_topk(q, sub_keys):
    t, qw = q.shape
    heads, _, nkeys, dk2 = sub_keys.shape
    tt = _tile(t, 256)
    hk = heads * PEER_TOPK
    out = jax.ShapeDtypeStruct((hk, t), I32)
    spec = pl.BlockSpec((PEER_TOPK, tt), lambda i, h: (h, i))
    return pl.pallas_call(
        functools.partial(_peer_topk_kernel, topk=PEER_TOPK),
        name="peer_topk",
        grid=(t // tt, heads),
        in_specs=[pl.BlockSpec((tt, 2 * dk2), lambda i, h: (i, h)),
                  pl.BlockSpec((None, 2, nkeys, dk2), lambda i, h: (h, 0, 0, 0))],
        out_specs=[spec, spec, spec],
        out_shape=[out, out, jax.ShapeDtypeStruct((hk, t), F32)],
        compiler_params=_params("parallel", "parallel"),
    )(q, sub_keys.astype(BF16))


def _peer_gates_kernel(i1_ref, i2_ref, g_ref, o_ref, i1s_ref, i2s_ref, gs_ref):
    nk = o_ref.shape[2]
    i1s_ref[...] = i1_ref[...].astype(F32).T
    i2s_ref[...] = i2_ref[...].astype(F32).T
    gs_ref[...] = g_ref[...].T
    key = lax.broadcasted_iota(I32, (nk, i1_ref.shape[0]), 0).astype(F32)

    def body(t, _):
        r1 = i1s_ref[pl.ds(t, 1), :]
        r2 = i2s_ref[pl.ds(t, 1), :]
        gg = gs_ref[pl.ds(t, 1), :]
        a_mat = jnp.where(key == r1, gg, 0.0).astype(BF16)
        b_mat = jnp.where(key == r2, 1.0, 0.0).astype(BF16)
        gmap = lax.dot_general(a_mat, b_mat, (((1,), (1,)), ((), ())), preferred_element_type=F32)
        o_ref[:, pl.ds(pl.multiple_of(t * SUBLANES, SUBLANES), SUBLANES), :] = gmap.reshape(
            nk // SUBLANES, SUBLANES, nk)
        return 0

    lax.fori_loop(0, i1s_ref.shape[0], body, 0, unroll=GATES_UNROLL)


def _peer_gates(i1t, i2t, gt, nkeys):
    hk, t = i1t.shape
    tt = _tile(t, 128)
    spec = pl.BlockSpec((hk, tt), lambda i: (0, i))
    return pl.pallas_call(
        _peer_gates_kernel,
        name="peer_gates",
        grid=(t // tt,),
        in_specs=[spec, spec, spec],
        out_specs=pl.BlockSpec((nkeys // SUBLANES, tt * SUBLANES, nkeys), lambda i: (0, i, 0)),
        out_shape=jax.ShapeDtypeStruct((nkeys // SUBLANES, t * SUBLANES, nkeys), F32),
        scratch_shapes=[pltpu.VMEM((tt, hk), F32)] * 3,
        compiler_params=_params("parallel"),
    )(i1t, i2t, gt)


def _peer_ffn_kernel(x_ref, dn_ref, up_ref, gd_ref, o_ref, s_ref):
    j = pl.program_id(1)
    tm = x_ref.shape[0]
    nk = gd_ref.shape[1]
    per = SUBLANES // 2
    half = o_ref.shape[1] // 2

    @pl.when(j == 0)
    def _():
        o_ref[...] = jnp.zeros(o_ref.shape, F32)
        s_ref[1] = jnp.zeros(s_ref.shape[1:], F32)

    def step(prev):
        pieces = []
        for c in range(per):
            sc = s_ref[prev, :, c * nk:(c + 1) * nk]
            act = 0.5 * sc * (1.0 + lax.erf(sc * (2.0 ** -0.5)))
            gate = gd_ref[pl.ds(prev * per + c, tm, stride=SUBLANES), :]
            pieces.append((act * gate).astype(BF16))
        a = jnp.concatenate(pieces, axis=1)
        s_new = jnp.dot(x_ref[...], dn_ref[...], preferred_element_type=F32)
        for n in range(2):
            cols = slice(n * half, (n + 1) * half)
            o_ref[:, cols] += jnp.dot(a, up_ref[:, cols], preferred_element_type=F32)
        s_ref[1 - prev] = s_new

    for prev in range(2):
        pl.when(lax.rem(j + 1, 2) == prev)(functools.partial(step, prev))


def _peer_ffn(n2, down, up, gd3):
    t, d = n2.shape
    ne = up.shape[0]
    nk = gd3.shape[2]
    tm = _tile(t, 1024)
    te = SUBLANES * nk // 2
    nj = ne // te
    assert nj % 2 == 0
    down_t = down.reshape(nj, te, d).transpose(0, 2, 1)
    last = lambda j: jnp.maximum(j - 1, 0)
    return pl.pallas_call(
        _peer_ffn_kernel,
        name="peer_ffn",
        grid=(t // tm, nj + 1),
        in_specs=[pl.BlockSpec((tm, d), lambda i, j: (i, 0)),
                  pl.BlockSpec((None, d, te), lambda i, j: (jnp.minimum(j, nj - 1), 0, 0)),
                  pl.BlockSpec((te, d), lambda i, j: (last(j), 0)),
                  pl.BlockSpec((None, tm * SUBLANES, nk), lambda i, j: (last(j) // 2, i, 0))],
        out_specs=pl.BlockSpec((tm, d), lambda i, j: (i, 0)),
        out_shape=jax.ShapeDtypeStruct((t, d), F32),
        scratch_shapes=[pltpu.VMEM((2, tm, te), F32)],
        compiler_params=_params("parallel", "arbitrary", vmem=VMEM_LIMIT_FFN),
    )(n2, down_t, up, gd3)


def _hybrid_mixer(h, n, l, bsz, seq, w_in, conv_a_w, ssd_conv_w, ssd_conv_b, ssd_dt_bias,
                  ssd_a_log, ssd_d, ssd_norm_w, w_branch_a, w_branch_b, w_branch_c, w_out):
    d = h.shape[1]
    conv_w = conv_a_w.shape[2]
    sb_w = w_branch_b.shape[1]
    ssd_w = w_branch_c.shape[1]
    xbc_w = ssd_conv_w.shape[2]
    heads = ssd_d.shape[1]
    bounds = [0, 3 * conv_w, 3 * sb_w, ssd_w, xbc_w, heads, 3 * d]
    offs = [sum(bounds[:k + 1]) for k in range(len(bounds))]
    w = w_in[l]
    w_a, w_b, w_z, w_xbc, w_dt, w_g = (w[:, offs[k]:offs[k + 1]].astype(BF16) for k in range(6))
    w_dt = jnp.pad(w_dt, ((0, 0), (0, LANES - heads)))

    u_a = _matmul(n, w_a, BF16)
    u_b = _matmul(n, w_b, BF16)
    u_z = _matmul(n, w_z, BF16)
    u_xbc = _matmul(n, w_xbc, BF16)
    u_dt = _matmul(n, w_dt, F32)
    u_g = _matmul(n, w_g, BF16)

    mix_a = _short_conv(u_a, conv_a_w[l], bsz, seq)
    mix_b = _sb_attention(u_b, bsz, seq)
    xbc_act = _ssd_conv(u_xbc, ssd_conv_w[l], ssd_conv_b[l], bsz, seq)
    mix_c = _ssd(xbc_act, u_z, u_dt, ssd_dt_bias[l], ssd_a_log[l], ssd_d[l], ssd_norm_w[l], bsz, seq)

    m = _matmul(mix_a, w_branch_a[l].astype(BF16), F32, gate=u_g)
    m = _matmul(mix_b, w_branch_b[l].astype(BF16), F32, gate=u_g, gate_col=d, add=m)
    m = _matmul(mix_c, w_branch_c[l].astype(BF16), BF16, gate=u_g, gate_col=2 * d, add=m, tm=512)
    return _matmul(m, w_out[l].astype(BF16), F32, add=h)


def _peer(h, l, norm_ffn_w, peer_w_query, peer_sub_keys, peer_down, peer_up):
    n2 = _rmsnorm(h, norm_ffn_w[l], BF16)
    q = _matmul(n2, peer_w_query[l].astype(BF16), BF16)
    i1t, i2t, gt = _peer_topk(q, peer_sub_keys[l])
    gd3 = _peer_gates(i1t, i2t, gt, peer_sub_keys.shape[3])
    return _peer_ffn(n2, peer_down[l].astype(BF16), peer_up[l].astype(BF16), gd3)


def kernel(x, norm_mix_w, w_in, conv_a_w, ssd_conv_w, ssd_conv_b, ssd_dt_bias, ssd_a_log, ssd_d,
           ssd_norm_w, w_branch_a, w_branch_b, w_branch_c, w_out, norm_ffn_w, peer_w_query,
           peer_sub_keys, peer_down, peer_up, final_norm_w):
    bsz, seq, d = x.shape
    h = x.reshape(bsz * seq, d)
    ffn = None
    for l in range(w_in.shape[0]):
        if ffn is None:
            n = _rmsnorm(h, norm_mix_w[l], BF16)
        else:
            h, n = _rmsnorm(h, norm_mix_w[l], BF16, add=ffn)
        h = _hybrid_mixer(h, n, l, bsz, seq, w_in, conv_a_w, ssd_conv_w, ssd_conv_b,
                          ssd_dt_bias, ssd_a_log, ssd_d, ssd_norm_w, w_branch_a, w_branch_b,
                          w_branch_c, w_out)
        ffn = _peer(h, l, norm_ffn_w, peer_w_query, peer_sub_keys, peer_down, peer_up)
    return _rmsnorm(h, final_norm_w, x.dtype, add=ffn, keep_sum=False).reshape(bsz, seq, d)
```

```python
import functools

import jax
import jax.numpy as jnp
from jax import lax
from jax.experimental import pallas as pl
from jax.experimental.pallas import tpu as pltpu

F32 = jnp.float32
BF16 = jnp.bfloat16
I32 = jnp.int32

EPS = 1e-6
LOG2E = 1.4426950408889634
SB_HEAD_DIM = 128
SSD_HEAD_DIM = 64
SSD_GROUPS = 8
SSD_STATE = 128
SSD_CHUNK = 128
SSD_CHUNKS_PER_STEP = 4
GATES_UNROLL = 64
PEER_HEADS = 8
PEER_TOPK = 16

LANES = 128
SUBLANES = 8
VMEM_LIMIT = 48 * 1024 * 1024
VMEM_LIMIT_FFN = 56 * 1024 * 1024


def _params(*sem, vmem=VMEM_LIMIT):
    return pltpu.CompilerParams(dimension_semantics=sem, vmem_limit_bytes=vmem)


def _log2(n):
    assert n & (n - 1) == 0
    return n.bit_length() - 1


def _tile(n, pref):
    if n <= pref:
        return n
    t = pref
    while n % t:
        t -= SUBLANES
    return t


def _rmsnorm_kernel(x_ref, w_ref, o_ref):
    x = x_ref[...].astype(F32)
    ms = jnp.mean(x * x, axis=-1, keepdims=True)
    o_ref[...] = (x * lax.rsqrt(ms + EPS) * w_ref[...]).astype(o_ref.dtype)


def _add_rmsnorm_kernel(x_ref, a_ref, w_ref, *out_refs):
    h = x_ref[...] + a_ref[...]
    if len(out_refs) == 2:
        out_refs[0][...] = h
    ms = jnp.mean(h * h, axis=-1, keepdims=True)
    out_refs[-1][...] = (h * lax.rsqrt(ms + EPS) * w_ref[...]).astype(out_refs[-1].dtype)


def _rmsnorm(x, w, out_dtype, add=None, keep_sum=True):
    t, d = x.shape
    tm = _tile(t, 512)
    row = pl.BlockSpec((tm, d), lambda i: (i, 0))
    wspec = pl.BlockSpec((1, d), lambda i: (0, 0))
    normed = jax.ShapeDtypeStruct((t, d), out_dtype)
    w2 = w.reshape(1, d).astype(F32)
    if add is None:
        return pl.pallas_call(
            _rmsnorm_kernel,
            name="rmsnorm",
            grid=(t // tm,),
            in_specs=[row, wspec],
            out_specs=row,
            out_shape=normed,
            compiler_params=_params("parallel"),
        )(x, w2)
    return pl.pallas_call(
        _add_rmsnorm_kernel,
        name="add_rmsnorm",
        grid=(t // tm,),
        in_specs=[row, row, wspec],
        out_specs=[row, row] if keep_sum else row,
        out_shape=[jax.ShapeDtypeStruct((t, d), F32), normed] if keep_sum else normed,
        compiler_params=_params("parallel"),
    )(x, add, w2)


def _mm_kernel(*refs, has_gate, has_add):
    x_ref, w_ref = refs[0], refs[1]
    o_ref = refs[-1]
    acc = jnp.dot(x_ref[...], w_ref[...], preferred_element_type=F32)
    k = 2
    if has_gate:
        acc = acc * jax.nn.sigmoid(refs[k][...].astype(F32))
        k += 1
    if has_add:
        acc = acc + refs[k][...].astype(F32)
    o_ref[...] = acc.astype(o_ref.dtype)


def _matmul(x, w, out_dtype, gate=None, gate_col=0, add=None, tm=1024, tn=1024, name="matmul"):
    m, kd = x.shape
    n = w.shape[1]
    tm = _tile(m, tm)
    tn = min(tn, n)
    assert n % tn == 0 and gate_col % tn == 0
    goff = gate_col // tn
    ins = [x, w]
    specs = [pl.BlockSpec((tm, kd), lambda i, j: (i, 0)),
             pl.BlockSpec((kd, tn), lambda i, j: (0, j))]
    if gate is not None:
        ins.append(gate)
        specs.append(pl.BlockSpec((tm, tn), lambda i, j: (i, goff + j)))
    if add is not None:
        ins.append(add)
        specs.append(pl.BlockSpec((tm, tn), lambda i, j: (i, j)))
    return pl.pallas_call(
        functools.partial(_mm_kernel, has_gate=gate is not None, has_add=add is not None),
        name=name,
        grid=(m // tm, n // tn),
        in_specs=specs,
        out_specs=pl.BlockSpec((tm, tn), lambda i, j: (i, j)),
        out_shape=jax.ShapeDtypeStruct((m, n), out_dtype),
        compiler_params=_params("parallel", "parallel"),
    )(*ins)


def _conv_taps(cur, buf_ref, w_ref, first):
    ts = cur.shape[0]
    kw = w_ref.shape[0]

    @pl.when(first)
    def _():
        buf_ref[0:SUBLANES, :] = jnp.zeros((SUBLANES, cur.shape[1]), F32)

    buf_ref[SUBLANES:SUBLANES + ts, :] = cur
    y = cur * w_ref[kw - 1:kw, :]
    for d in range(1, kw):
        y = y + buf_ref[SUBLANES - d:SUBLANES - d + ts, :] * w_ref[kw - 1 - d:kw - d, :]
    buf_ref[0:SUBLANES, :] = cur[ts - SUBLANES:ts, :]
    return y


def _short_conv_kernel(b_ref, c_ref, h_ref, w_ref, o_ref, buf_ref):
    cur = c_ref[...].astype(F32) * h_ref[...].astype(F32)
    y = _conv_taps(cur, buf_ref, w_ref, pl.program_id(2) == 0)
    o_ref[...] = (b_ref[...].astype(F32) * y).astype(o_ref.dtype)


def _short_conv(u_a, conv_w, bsz, seq):
    t = u_a.shape[0]
    width = conv_w.shape[1]
    ts = _tile(seq, 512)
    tc = 512
    nc = width // tc
    ns = seq // ts
    row = lambda b, j, s: b * ns + s
    return pl.pallas_call(
        _short_conv_kernel,
        name="short_conv",
        grid=(bsz, nc, ns),
        in_specs=[pl.BlockSpec((ts, tc), lambda b, j, s: (row(b, j, s), j)),
                  pl.BlockSpec((ts, tc), lambda b, j, s: (row(b, j, s), nc + j)),
                  pl.BlockSpec((ts, tc), lambda b, j, s: (row(b, j, s), 2 * nc + j)),
                  pl.BlockSpec((conv_w.shape[0], tc), lambda b, j, s: (0, j))],
        out_specs=pl.BlockSpec((ts, tc), lambda b, j, s: (row(b, j, s), j)),
        out_shape=jax.ShapeDtypeStruct((t, width), BF16),
        scratch_shapes=[pltpu.VMEM((SUBLANES + ts, tc), F32)],
        compiler_params=_params("parallel", "parallel", "arbitrary"),
    )(u_a, u_a, u_a, conv_w.astype(F32))


def _ssd_conv_kernel(x_ref, w_ref, b_ref, o_ref, buf_ref):
    y = _conv_taps(x_ref[...].astype(F32), buf_ref, w_ref, pl.program_id(2) == 0) + b_ref[...]
    o_ref[...] = (y * jax.nn.sigmoid(y)).astype(o_ref.dtype)


def _ssd_conv(xbc, conv_w, conv_b, bsz, seq):
    t, width = xbc.shape
    ts = _tile(seq, 512)
    tc = 512
    nc = width // tc
    ns = seq // ts
    return pl.pallas_call(
        _ssd_conv_kernel,
        name="ssd_conv",
        grid=(bsz, nc, ns),
        in_specs=[pl.BlockSpec((ts, tc), lambda b, j, s: (b * ns + s, j)),
                  pl.BlockSpec((conv_w.shape[0], tc), lambda b, j, s: (0, j)),
                  pl.BlockSpec((1, tc), lambda b, j, s: (0, j))],
        out_specs=pl.BlockSpec((ts, tc), lambda b, j, s: (b * ns + s, j)),
        out_shape=jax.ShapeDtypeStruct((t, width), BF16),
        scratch_shapes=[pltpu.VMEM((SUBLANES + ts, tc), F32)],
        compiler_params=_params("parallel", "parallel", "arbitrary"),
    )(xbc, conv_w.astype(F32), conv_b.reshape(1, width).astype(F32))


def _split2(x):
    hi = x.astype(BF16)
    lo = (x - hi.astype(F32)).astype(BF16)
    return hi, lo


def _split3(x):
    hi = x.astype(BF16)
    r = x - hi.astype(F32)
    mid = r.astype(BF16)
    lo = (r - mid.astype(F32)).astype(BF16)
    return hi, mid, lo


def _sb_attn_kernel(q_ref, k_ref, v_ref, o_ref, qs_ref, oacc_ref, run_ref, *, bq, bk, scale):
    i = pl.program_id(2)
    dh = SB_HEAD_DIM
    nh = q_ref.shape[1] // dh
    nkb = bq // bk
    krow = lax.broadcasted_iota(I32, (bk, bk), 0)
    kcol = lax.broadcasted_iota(I32, (bk, bk), 1)
    later_mat = jnp.where(krow > kcol, 1.0, 0.0).astype(BF16)
    for h in range(nh):
        qs_ref[h] = (q_ref[:, h * dh:(h + 1) * dh].astype(F32) * (scale * LOG2E)).astype(BF16)

    def tile(h, start, run, mask, r0=0):
        ks = k_ref[pl.ds(start, bk), h * dh:(h + 1) * dh]
        vs = v_ref[pl.ds(start, bk), h * dh:(h + 1) * dh]
        z = lax.dot_general(qs_ref[h, r0:, :], ks, (((1,), (1,)), ((), ())), preferred_element_type=F32)
        z = z.astype(BF16)
        neg = jnp.minimum(z, 0.0)
        pos = z - neg
        ls = neg - jnp.log(1.0 + jnp.exp2(neg - pos)) * LOG2E
        lnb = ls - z
        if mask is not None:
            lnb = jnp.where(mask, lnb.astype(F32), 0.0).astype(BF16)
        later = jnp.dot(lnb, later_mat, preferred_element_type=F32)
        a = jnp.exp2(ls.astype(F32) + later + jnp.concatenate([run] * (bk // LANES), axis=1))
        if mask is not None:
            a = jnp.where(mask, a, 0.0)
        o = jnp.dot(a.astype(BF16), vs, preferred_element_type=F32)
        total = later[:, 0:1] + lnb[:, 0:1].astype(F32)
        return o, run + jnp.broadcast_to(total, run.shape)

    oacc_ref[...] = jnp.zeros(oacc_ref.shape, F32)
    run_ref[...] = jnp.zeros(run_ref.shape, F32)
    for d in range(nkb):
        r0 = (nkb - 1 - d) * bk
        start = pl.multiple_of(i * bq + r0, bk)
        mask = (lax.broadcasted_iota(I32, (bq - r0, bk), 1) < lax.broadcasted_iota(I32, (bq - r0, bk), 0))
        for h in range(nh):
            o, run = tile(h, start, run_ref[h, r0:, :], mask, r0)
            oacc_ref[h, r0:, :] += o
            run_ref[h, r0:, :] = run
    state = [(oacc_ref[h], run_ref[h]) for h in range(nh)]

    def body(jj, carry):
        out = list(carry)
        for d in range(nkb):
            start = pl.multiple_of(((i - jj) * nkb - 1 - d) * bk, bk)
            for h in range(nh):
                o, run = tile(h, start, out[h][1], None)
                out[h] = (out[h][0] + o, run)
        return tuple(out)

    res = lax.fori_loop(0, i, body, tuple(state))
    o_ref[...] = jnp.concatenate([r[0] for r in res], axis=1).astype(o_ref.dtype)


SB_HEADS_PER_STEP = 2
SB_Q_BLOCK = 1024
SB_K_BLOCK = 256


def _sb_attention(u_b, bsz, seq):
    t, w3 = u_b.shape
    width = w3 // 3
    hw = SB_HEADS_PER_STEP * SB_HEAD_DIM
    groups = width // hw
    bq = _tile(seq, SB_Q_BLOCK)
    bk = min(SB_K_BLOCK, bq)
    assert bq % bk == 0
    nq = seq // bq
    return pl.pallas_call(
        functools.partial(_sb_attn_kernel, bq=bq, bk=bk, scale=SB_HEAD_DIM ** -0.5),
        grid=(bsz, groups, nq),
        in_specs=[pl.BlockSpec((bq, hw), lambda b, h, i: (b * nq + i, h)),
                  pl.BlockSpec((seq, hw), lambda b, h, i: (b, groups + h)),
                  pl.BlockSpec((seq, hw), lambda b, h, i: (b, 2 * groups + h))],
        out_specs=pl.BlockSpec((bq, hw), lambda b, h, i: (b * nq + i, h)),
        out_shape=jax.ShapeDtypeStruct((t, width), BF16),
        scratch_shapes=[pltpu.VMEM((SB_HEADS_PER_STEP, bq, SB_HEAD_DIM), BF16),
                        pltpu.VMEM((SB_HEADS_PER_STEP, bq, SB_HEAD_DIM), F32),
                        pltpu.VMEM((SB_HEADS_PER_STEP, bq, LANES), F32)],
        name="sb_attention",
        compiler_params=_params("parallel", "parallel", "arbitrary"),
    )(u_b, u_b, u_b)


def _ssd_kernel(xs_ref, bm_ref, cm_ref, z_ref, dt_ref, dtb_ref, alog_ref, dsk_ref, nw_ref,
                o_ref, state_ref, acst_ref, *, hpg, hd, lc):
    g = pl.program_id(1)
    gw = hpg * hd
    nsub = xs_ref.shape[0] // lc

    @pl.when(pl.program_id(2) == 0)
    def _():
        state_ref[...] = jnp.zeros(state_ref.shape, F32)

    row = lax.broadcasted_iota(I32, (lc, lc), 0)
    col = lax.broadcasted_iota(I32, (lc, lc), 1)
    causal = row >= col
    tri = jnp.where(causal, 1.0, 0.0).astype(BF16)
    hrow = lax.broadcasted_iota(I32, (LANES, gw), 0)
    hcol = lax.broadcasted_iota(I32, (LANES, gw), 1)
    spread = jnp.where(hrow == g * hpg + (hcol >> _log2(hd)), 1.0, 0.0).astype(BF16)
    hrow2 = lax.broadcasted_iota(I32, (LANES, hpg * lc), 0)
    hcol2 = lax.broadcasted_iota(I32, (LANES, hpg * lc), 1)
    spread_l = jnp.where(hrow2 == g * hpg + (hcol2 >> _log2(lc)), 1.0, 0.0).astype(BF16)

    def expand(x, parts, mat):
        pieces = (_split3(x) if parts == 3 else _split2(x) if parts == 2 else (x.astype(BF16),))
        return sum(jnp.dot(p, mat, preferred_element_type=F32) for p in pieces)

    neg_a = -jnp.exp(alog_ref[...])
    dsk_x = expand(jnp.broadcast_to(dsk_ref[...], (SUBLANES, LANES)), 3, spread)[0:1, :]

    def chunk(sc, prev):
        rows = slice(sc * lc, (sc + 1) * lc)
        dt = jax.nn.softplus(dt_ref[rows, :] + dtb_ref[...])
        a_cs = sum(jnp.dot(tri, part, preferred_element_type=F32) for part in _split3(dt * neg_a))
        acst_ref[sc] = a_cs.T
        exp_acs = jnp.exp(a_cs)
        dte = jnp.exp(a_cs[lc - 1:lc, :] - a_cs)

        dt_x = expand(dt, 1, spread)
        dte_x = expand(dte, 1, spread)
        eacs_x = expand(exp_acs, 2, spread)
        acs_col = expand(a_cs, 2, spread_l)

        xs = xs_ref[rows, :].astype(F32)
        bm = bm_ref[rows, :]
        cm = cm_ref[rows, :]
        xdt = xs * dt_x
        xdt_b = xdt.astype(BF16)
        cb = lax.dot_general(cm, bm, (((1,), (1,)), ((), ())), preferred_element_type=F32)

        y_parts = []
        for r in range(hpg):
            a_l = acs_col[:, r * lc:(r + 1) * lc]
            a_s = acst_ref[sc, pl.ds(g * hpg + r, 1), :]
            seg = jnp.where(causal, jnp.exp(jnp.where(causal, a_l - a_s, 0.0)), 0.0)
            m = (cb * seg).astype(BF16)
            y_parts.append(jnp.dot(m, xdt_b[:, r * hd:(r + 1) * hd], preferred_element_type=F32))
        y_diag = jnp.concatenate(y_parts, axis=1)

        y_off = jnp.dot(cm, prev.astype(BF16), preferred_element_type=F32) * eacs_x
        y = y_diag + y_off + xs * dsk_x

        bm_t = bm.astype(F32).T.astype(BF16)
        new_state = jnp.dot(bm_t, (xdt * dte_x).astype(BF16), preferred_element_type=F32)

        zf = z_ref[rows, :].astype(F32)
        yg = y * (zf * jax.nn.sigmoid(zf))
        ms = jnp.mean(yg * yg, axis=-1, keepdims=True)
        o_ref[rows, :] = (yg * lax.rsqrt(ms + EPS) * nw_ref[...]).astype(o_ref.dtype)
        return prev * eacs_x[lc - 1:lc, :] + new_state

    state = state_ref[...]
    for sc in range(nsub):
        state = chunk(sc, state)
    state_ref[...] = state


def _pad_lanes(v):
    return jnp.zeros((1, LANES), F32).at[0, :v.shape[0]].set(v.astype(F32))


def _ssd(xbc_act, z, dt_raw, dt_bias, a_log, d_skip, norm_w, bsz, seq):
    t, width = z.shape
    heads = width // SSD_HEAD_DIM
    hpg = heads // SSD_GROUPS
    gw = hpg * SSD_HEAD_DIM
    nsub = SSD_CHUNKS_PER_STEP if seq % (SSD_CHUNKS_PER_STEP * SSD_CHUNK) == 0 else 1
    lc = nsub * SSD_CHUNK
    nch = seq // lc
    nbs = width // SSD_STATE
    assert gw * SSD_GROUPS == width and SSD_STATE == LANES and heads <= LANES
    row = lambda b, g, c: b * nch + c
    return pl.pallas_call(
        functools.partial(_ssd_kernel, hpg=hpg, hd=SSD_HEAD_DIM, lc=SSD_CHUNK),
        name="ssd",
        grid=(bsz, SSD_GROUPS, nch),
        in_specs=[pl.BlockSpec((lc, gw), lambda b, g, c: (row(b, g, c), g)),
                  pl.BlockSpec((lc, SSD_STATE), lambda b, g, c: (row(b, g, c), nbs + g)),
                  pl.BlockSpec((lc, SSD_STATE), lambda b, g, c: (row(b, g, c), nbs + SSD_GROUPS + g)),
                  pl.BlockSpec((lc, gw), lambda b, g, c: (row(b, g, c), g)),
                  pl.BlockSpec((lc, LANES), lambda b, g, c: (row(b, g, c), 0)),
                  pl.BlockSpec((1, LANES), lambda b, g, c: (0, 0)),
                  pl.BlockSpec((1, LANES), lambda b, g, c: (0, 0)),
                  pl.BlockSpec((1, LANES), lambda b, g, c: (0, 0)),
                  pl.BlockSpec((1, gw), lambda b, g, c: (0, g))],
        out_specs=pl.BlockSpec((lc, gw), lambda b, g, c: (row(b, g, c), g)),
        out_shape=jax.ShapeDtypeStruct((t, width), BF16),
        scratch_shapes=[pltpu.VMEM((SSD_STATE, gw), F32), pltpu.VMEM((nsub, LANES, SSD_CHUNK), F32)],
        compiler_params=_params("parallel", "parallel", "arbitrary"),
    )(xbc_act, xbc_act, xbc_act, z, dt_raw, _pad_lanes(dt_bias), _pad_lanes(a_log),
      _pad_lanes(d_skip), norm_w.reshape(1, width).astype(F32))


def _topk_rows(s, k):
    n = s.shape[0]
    iota = lax.broadcasted_iota(I32, s.shape, 0)
    vals, idxs = [], []
    for it in range(k):
        m = jnp.max(s, axis=0, keepdims=True)
        i = jnp.min(jnp.where(s == m, iota, n), axis=0, keepdims=True)
        vals.append(m)
        idxs.append(i)
        if it + 1 < k:
            s = jnp.where(iota == i, -jnp.inf, s)
    return jnp.concatenate(vals, axis=0), jnp.concatenate(idxs, axis=0)


def _peer_topk_kernel(q_ref, keys_ref, i1_ref, i2_ref, g_ref, *, topk):
    half = q_ref.shape[1] // 2
    sub_v, sub_i = [], []
    for c in range(2):
        s = lax.dot_general(keys_ref[c], q_ref[:, c * half:(c + 1) * half],
                            (((1,), (1,)), ((), ())), preferred_element_type=F32)
        v, i = _topk_rows(s, topk)
        sub_v.append(v)
        sub_i.append(i)
    n_wide = topk // 2
    pieces, starts = [], []
    for a in range(n_wide):
        nb = min(topk, -(-(topk // (a + 1)) // SUBLANES) * SUBLANES)
        starts.append(sum(p.shape[0] for p in pieces))
        pieces.append(sub_v[0][a:a + 1, :] + sub_v[1][0:nb, :])
    tail_start = sum(p.shape[0] for p in pieces)
    pieces.append(sub_v[0][n_wide:topk, :] + sub_v[1][0:1, :])
    top_s, pos = _topk_rows(jnp.concatenate(pieces, axis=0), topk)
    pa = jnp.zeros(pos.shape, I32)
    pb = pos
    for a in range(1, n_wide):
        ge = pos >= starts[a]
        pa = jnp.where(ge, a, pa)
        pb = jnp.where(ge, pos - starts[a], pb)
    ge = pos >= tail_start
    pa = jnp.where(ge, pos + (n_wide - tail_start), pa)
    pb = jnp.where(ge, 0, pb)
    i1 = jnp.zeros(pos.shape, I32)
    i2 = jnp.zeros(pos.shape, I32)
    for a in range(topk):
        i1 = jnp.where(pa == a, sub_i[0][a:a + 1, :], i1)
        i2 = jnp.where(pb == a, sub_i[1][a:a + 1, :], i2)
    e = jnp.exp(top_s - top_s[0:1, :])
    i1_ref[...] = i1
    i2_ref[...] = i2
    g_ref[...] = e / jnp.sum(e, axis=0, keepdims=True)


def _peer_topk(q, sub_keys):
    t, qw = q.shape
    heads, _, nkeys, dk2 = sub_keys.shape
    tt = _tile(t, 256)
    hk = heads * PEER_TOPK
    out = jax.ShapeDtypeStruct((hk, t), I32)
    spec = pl.BlockSpec((PEER_TOPK, tt), lambda i, h: (h, i))
    return pl.pallas_call(
        functools.partial(_peer_topk_kernel, topk=PEER_TOPK),
        name="peer_topk",
        grid=(t // tt, heads),
        in_specs=[pl.BlockSpec((tt, 2 * dk2), lambda i, h: (i, h)),
                  pl.BlockSpec((None, 2, nkeys, dk2), lambda i, h: (h, 0, 0, 0))],
        out_specs=[spec, spec, spec],
        out_shape=[out, out, jax.ShapeDtypeStruct((hk, t), F32)],
        compiler_params=_params("parallel", "parallel"),
    )(q, sub_keys.astype(BF16))


def _peer_gates_kernel(i1_ref, i2_ref, g_ref, o_ref, i1s_ref, i2s_ref, gs_ref):
    nk = o_ref.shape[2]
    i1s_ref[...] = i1_ref[...].astype(F32).T
    i2s_ref[...] = i2_ref[...].astype(F32).T
    gs_ref[...] = g_ref[...].T
    key = lax.broadcasted_iota(I32, (nk, i1_ref.shape[0]), 0).astype(F32)

    def body(t, _):
        r1 = i1s_ref[pl.ds(t, 1), :]
        r2 = i2s_ref[pl.ds(t, 1), :]
        gg = gs_ref[pl.ds(t, 1), :]
        a_mat = jnp.where(key == r1, gg, 0.0).astype(BF16)
        b_mat = jnp.where(key == r2, 1.0, 0.0).astype(BF16)
        gmap = lax.dot_general(a_mat, b_mat, (((1,), (1,)), ((), ())), preferred_element_type=F32)
        o_ref[:, pl.ds(pl.multiple_of(t * SUBLANES, SUBLANES), SUBLANES), :] = gmap.reshape(
            nk // SUBLANES, SUBLANES, nk)
        return 0

    lax.fori_loop(0, i1s_ref.shape[0], body, 0, unroll=GATES_UNROLL)


def _peer_gates(i1t, i2t, gt, nkeys):
    hk, t = i1t.shape
    tt = _tile(t, 128)
    spec = pl.BlockSpec((hk, tt), lambda i: (0, i))
    return pl.pallas_call(
        _peer_gates_kernel,
        name="peer_gates",
        grid=(t // tt,),
        in_specs=[spec, spec, spec],
        out_specs=pl.BlockSpec((nkeys // SUBLANES, tt * SUBLANES, nkeys), lambda i: (0, i, 0)),
        out_shape=jax.ShapeDtypeStruct((nkeys // SUBLANES, t * SUBLANES, nkeys), F32),
        scratch_shapes=[pltpu.VMEM((tt, hk), F32)] * 3,
        compiler_params=_params("parallel"),
    )(i1t, i2t, gt)


def _peer_ffn_kernel(x_ref, dn_ref, up_ref, gd_ref, o_ref):
    j = pl.program_id(1)
    tm = x_ref.shape[0]
    nk = gd_ref.shape[1]
    per = SUBLANES // 2
    half = o_ref.shape[1] // 2

    @pl.when(j == 0)
    def _():
        o_ref[...] = jnp.zeros(o_ref.shape, F32)

    def step(part):
        s = jnp.dot(x_ref[...], dn_ref[...], preferred_element_type=F32)
        pieces = []
        for c in range(per):
            sc = s[:, c * nk:(c + 1) * nk]
            act = 0.5 * sc * (1.0 + lax.erf(sc * (2.0 ** -0.5)))
            gate = gd_ref[pl.ds(part * per + c, tm, stride=SUBLANES), :]
            pieces.append((act * gate).astype(BF16))
        a = jnp.concatenate(pieces, axis=1)
        for n in range(2):
            cols = slice(n * half, (n + 1) * half)
            o_ref[:, cols] += jnp.dot(a, up_ref[:, cols], preferred_element_type=F32)

    for part in range(2):
        pl.when(lax.rem(j, 2) == part)(functools.partial(step, part))


def _peer_ffn(n2, down, up, gd3):
    t, d = n2.shape
    ne = up.shape[0]
    nk = gd3.shape[2]
    tm = _tile(t, 1024)
    te = SUBLANES * nk // 2
    nj = ne // te
    assert nj % 2 == 0
    down_t = down.reshape(nj, te, d).transpose(0, 2, 1)
    return pl.pallas_call(
        _peer_ffn_kernel,
        name="peer_ffn",
        grid=(t // tm, nj),
        in_specs=[pl.BlockSpec((tm, d), lambda i, j: (i, 0)),
                  pl.BlockSpec((None, d, te), lambda i, j: (j, 0, 0)),
                  pl.BlockSpec((te, d), lambda i, j: (j, 0)),
                  pl.BlockSpec((None, tm * SUBLANES, nk), lambda i, j: (j // 2, i, 0))],
        out_specs=pl.BlockSpec((tm, d), lambda i, j: (i, 0)),
        out_shape=jax.ShapeDtypeStruct((t, d), F32),
        compiler_params=_params("parallel", "arbitrary", vmem=VMEM_LIMIT_FFN),
    )(n2, down_t, up, gd3)


def _hybrid_mixer(h, n, l, bsz, seq, w_in, conv_a_w, ssd_conv_w, ssd_conv_b, ssd_dt_bias,
                  ssd_a_log, ssd_d, ssd_norm_w, w_branch_a, w_branch_b, w_branch_c, w_out):
    d = h.shape[1]
    conv_w = conv_a_w.shape[2]
    sb_w = w_branch_b.shape[1]
    ssd_w = w_branch_c.shape[1]
    xbc_w = ssd_conv_w.shape[2]
    heads = ssd_d.shape[1]
    bounds = [0, 3 * conv_w, 3 * sb_w, ssd_w, xbc_w, heads, 3 * d]
    offs = [sum(bounds[:k + 1]) for k in range(len(bounds))]
    w = w_in[l]
    w_a, w_b, w_z, w_xbc, w_dt, w_g = (w[:, offs[k]:offs[k + 1]].astype(BF16) for k in range(6))
    w_dt = jnp.pad(w_dt, ((0, 0), (0, LANES - heads)))

    u_a = _matmul(n, w_a, BF16)
    u_b = _matmul(n, w_b, BF16)
    u_z = _matmul(n, w_z, BF16)
    u_xbc = _matmul(n, w_xbc, BF16)
    u_dt = _matmul(n, w_dt, F32)
    u_g = _matmul(n, w_g, BF16)

    mix_a = _short_conv(u_a, conv_a_w[l], bsz, seq)
    mix_b = _sb_attention(u_b, bsz, seq)
    xbc_act = _ssd_conv(u_xbc, ssd_conv_w[l], ssd_conv_b[l], bsz, seq)
    mix_c = _ssd(xbc_act, u_z, u_dt, ssd_dt_bias[l], ssd_a_log[l], ssd_d[l], ssd_norm_w[l], bsz, seq)

    m = _matmul(mix_a, w_branch_a[l].astype(BF16), F32, gate=u_g)
    m = _matmul(mix_b, w_branch_b[l].astype(BF16), F32, gate=u_g, gate_col=d, add=m)
    m = _matmul(mix_c, w_branch_c[l].astype(BF16), BF16, gate=u_g, gate_col=2 * d, add=m, tm=512)
    return _matmul(m, w_out[l].astype(BF16), F32, add=h)


def _peer(h, l, norm_ffn_w, peer_w_query, peer_sub_keys, peer_down, peer_up):
    n2 = _rmsnorm(h, norm_ffn_w[l], BF16)
    q = _matmul(n2, peer_w_query[l].astype(BF16), BF16)
    i1t, i2t, gt = _peer_topk(q, peer_sub_keys[l])
    gd3 = _peer_gates(i1t, i2t, gt, peer_sub_keys.shape[3])
    return _peer_ffn(n2, peer_down[l].astype(BF16), peer_up[l].astype(BF16), gd3)


def kernel(x, norm_mix_w, w_in, conv_a_w, ssd_conv_w, ssd_conv_b, ssd_dt_bias, ssd_a_log, ssd_d,
           ssd_norm_w, w_branch_a, w_branch_b, w_branch_c, w_out, norm_ffn_w, peer_w_query,
           peer_sub_keys, peer_down, peer_up, final_norm_w):
    bsz, seq, d = x.shape
    h = x.reshape(bsz * seq, d)
    ffn = None
    for l in range(w_in.shape[0]):
        if ffn is None:
            n = _rmsnorm(h, norm_mix_w[l], BF16)
        else:
            h, n = _rmsnorm(h, norm_mix_w[l], BF16, add=ffn)
        h = _hybrid_mixer(h, n, l, bsz, seq, w_in, conv_a_w, ssd_conv_w, ssd_conv_b,
                          ssd_dt_bias, ssd_a_log, ssd_d, ssd_norm_w, w_branch_a, w_branch_b,
                          w_branch_c, w_out)
        ffn = _peer(h, l, norm_ffn_w, peer_w_query, peer_sub_keys, peer_down, peer_up)
    return _rmsnorm(h, final_norm_w, x.dtype, add=ffn, keep_sum=False).reshape(bsz, seq, d)
```

```python
import functools

import jax
import jax.numpy as jnp
from jax import lax
from jax.experimental import pallas as pl
from jax.experimental.pallas import tpu as pltpu

F32 = jnp.float32
BF16 = jnp.bfloat16
I32 = jnp.int32

EPS = 1e-6
LOG2E = 1.4426950408889634
SB_HEAD_DIM = 128
SSD_HEAD_DIM = 64
SSD_GROUPS = 8
SSD_STATE = 128
SSD_CHUNK = 128
SSD_CHUNKS_PER_STEP = 4
GATES_UNROLL = 64
PEER_HEADS = 8
PEER_TOPK = 16

LANES = 128
SUBLANES = 8
VMEM_LIMIT = 48 * 1024 * 1024
VMEM_LIMIT_FFN = 56 * 1024 * 1024


def _params(*sem, vmem=VMEM_LIMIT):
    return pltpu.CompilerParams(dimension_semantics=sem, vmem_limit_bytes=vmem)


def _log2(n):
    assert n & (n - 1) == 0
    return n.bit_length() - 1


def _tile(n, pref):
    if n <= pref:
        return n
    t = pref
    while n % t:
        t -= SUBLANES
    return t


def _rmsnorm_kernel(x_ref, w_ref, o_ref):
    x = x_ref[...].astype(F32)
    ms = jnp.mean(x * x, axis=-1, keepdims=True)
    o_ref[...] = (x * lax.rsqrt(ms + EPS) * w_ref[...]).astype(o_ref.dtype)


def _add_rmsnorm_kernel(x_ref, a_ref, w_ref, *out_refs):
    h = x_ref[...] + a_ref[...]
    if len(out_refs) == 2:
        out_refs[0][...] = h
    ms = jnp.mean(h * h, axis=-1, keepdims=True)
    out_refs[-1][...] = (h * lax.rsqrt(ms + EPS) * w_ref[...]).astype(out_refs[-1].dtype)


def _rmsnorm(x, w, out_dtype, add=None, keep_sum=True):
    t, d = x.shape
    tm = _tile(t, 512)
    row = pl.BlockSpec((tm, d), lambda i: (i, 0))
    wspec = pl.BlockSpec((1, d), lambda i: (0, 0))
    normed = jax.ShapeDtypeStruct((t, d), out_dtype)
    w2 = w.reshape(1, d).astype(F32)
    if add is None:
        return pl.pallas_call(
            _rmsnorm_kernel,
            name="rmsnorm",
            grid=(t // tm,),
            in_specs=[row, wspec],
            out_specs=row,
            out_shape=normed,
            compiler_params=_params("parallel"),
        )(x, w2)
    return pl.pallas_call(
        _add_rmsnorm_kernel,
        name="add_rmsnorm",
        grid=(t // tm,),
        in_specs=[row, row, wspec],
        out_specs=[row, row] if keep_sum else row,
        out_shape=[jax.ShapeDtypeStruct((t, d), F32), normed] if keep_sum else normed,
        compiler_params=_params("parallel"),
    )(x, add, w2)


def _mm_kernel(*refs, has_gate, has_add):
    x_ref, w_ref = refs[0], refs[1]
    o_ref = refs[-1]
    acc = jnp.dot(x_ref[...], w_ref[...], preferred_element_type=F32)
    k = 2
    if has_gate:
        acc = acc * jax.nn.sigmoid(refs[k][...].astype(F32))
        k += 1
    if has_add:
        acc = acc + refs[k][...].astype(F32)
    o_ref[...] = acc.astype(o_ref.dtype)


def _matmul(x, w, out_dtype, gate=None, gate_col=0, add=None, tm=1024, tn=1024, name="matmul"):
    m, kd = x.shape
    n = w.shape[1]
    tm = _tile(m, tm)
    tn = min(tn, n)
    assert n % tn == 0 and gate_col % tn == 0
    goff = gate_col // tn
    ins = [x, w]
    specs = [pl.BlockSpec((tm, kd), lambda i, j: (i, 0)),
             pl.BlockSpec((kd, tn), lambda i, j: (0, j))]
    if gate is not None:
        ins.append(gate)
        specs.append(pl.BlockSpec((tm, tn), lambda i, j: (i, goff + j)))
    if add is not None:
        ins.append(add)
        specs.append(pl.BlockSpec((tm, tn), lambda i, j: (i, j)))
    return pl.pallas_call(
        functools.partial(_mm_kernel, has_gate=gate is not None, has_add=add is not None),
        name=name,
        grid=(m // tm, n // tn),
        in_specs=specs,
        out_specs=pl.BlockSpec((tm, tn), lambda i, j: (i, j)),
        out_shape=jax.ShapeDtypeStruct((m, n), out_dtype),
        compiler_params=_params("parallel", "parallel"),
    )(*ins)


def _conv_taps(cur, buf_ref, w_ref, first):
    ts = cur.shape[0]
    kw = w_ref.shape[0]

    @pl.when(first)
    def _():
        buf_ref[0:SUBLANES, :] = jnp.zeros((SUBLANES, cur.shape[1]), F32)

    buf_ref[SUBLANES:SUBLANES + ts, :] = cur
    y = cur * w_ref[kw - 1:kw, :]
    for d in range(1, kw):
        y = y + buf_ref[SUBLANES - d:SUBLANES - d + ts, :] * w_ref[kw - 1 - d:kw - d, :]
    buf_ref[0:SUBLANES, :] = cur[ts - SUBLANES:ts, :]
    return y


def _short_conv_kernel(b_ref, c_ref, h_ref, w_ref, o_ref, buf_ref):
    cur = c_ref[...].astype(F32) * h_ref[...].astype(F32)
    y = _conv_taps(cur, buf_ref, w_ref, pl.program_id(2) == 0)
    o_ref[...] = (b_ref[...].astype(F32) * y).astype(o_ref.dtype)


def _short_conv(u_a, conv_w, bsz, seq):
    t = u_a.shape[0]
    width = conv_w.shape[1]
    ts = _tile(seq, 512)
    tc = 512
    nc = width // tc
    ns = seq // ts
    row = lambda b, j, s: b * ns + s
    return pl.pallas_call(
        _short_conv_kernel,
        name="short_conv",
        grid=(bsz, nc, ns),
        in_specs=[pl.BlockSpec((ts, tc), lambda b, j, s: (row(b, j, s), j)),
                  pl.BlockSpec((ts, tc), lambda b, j, s: (row(b, j, s), nc + j)),
                  pl.BlockSpec((ts, tc), lambda b, j, s: (row(b, j, s), 2 * nc + j)),
                  pl.BlockSpec((conv_w.shape[0], tc), lambda b, j, s: (0, j))],
        out_specs=pl.BlockSpec((ts, tc), lambda b, j, s: (row(b, j, s), j)),
        out_shape=jax.ShapeDtypeStruct((t, width), BF16),
        scratch_shapes=[pltpu.VMEM((SUBLANES + ts, tc), F32)],
        compiler_params=_params("parallel", "parallel", "arbitrary"),
    )(u_a, u_a, u_a, conv_w.astype(F32))


CONV_ROWS = 128
CONV_HALO = 16


def _ssd_conv_kernel(x_ref, w_ref, b_ref, o_ref, buf_ref):
    ts, tc = x_ref.shape
    kw = w_ref.shape[0]

    @pl.when(pl.program_id(2) == 0)
    def _():
        buf_ref[0:CONV_HALO, :] = jnp.zeros((CONV_HALO, tc), BF16)

    buf_ref[CONV_HALO:CONV_HALO + ts, :] = x_ref[...]
    row = lax.broadcasted_iota(I32, (CONV_ROWS, CONV_ROWS + CONV_HALO), 0)
    col = lax.broadcasted_iota(I32, (CONV_ROWS, CONV_ROWS + CONV_HALO), 1)
    shift = [jnp.where(col == row + (CONV_HALO - d), 1.0, 0.0).astype(BF16) for d in range(1, kw)]
    outs = []
    for r in range(ts // CONV_ROWS):
        ext = buf_ref[r * CONV_ROWS:(r + 1) * CONV_ROWS + CONV_HALO, :]
        y = ext[CONV_HALO:, :].astype(F32) * w_ref[kw - 1:kw, :] + b_ref[...]
        for d in range(1, kw):
            y = y + jnp.dot(shift[d - 1], ext, preferred_element_type=F32) * w_ref[kw - 1 - d:kw - d, :]
        outs.append((y * jax.nn.sigmoid(y)).astype(o_ref.dtype))
    o_ref[...] = jnp.concatenate(outs, axis=0)
    buf_ref[0:CONV_HALO, :] = x_ref[ts - CONV_HALO:ts, :]


def _ssd_conv(xbc, conv_w, conv_b, bsz, seq):
    t, width = xbc.shape
    ts = _tile(seq, 512)
    tc = 512
    nc = width // tc
    ns = seq // ts
    return pl.pallas_call(
        _ssd_conv_kernel,
        name="ssd_conv",
        grid=(bsz, nc, ns),
        in_specs=[pl.BlockSpec((ts, tc), lambda b, j, s: (b * ns + s, j)),
                  pl.BlockSpec((conv_w.shape[0], tc), lambda b, j, s: (0, j)),
                  pl.BlockSpec((1, tc), lambda b, j, s: (0, j))],
        out_specs=pl.BlockSpec((ts, tc), lambda b, j, s: (b * ns + s, j)),
        out_shape=jax.ShapeDtypeStruct((t, width), BF16),
        scratch_shapes=[pltpu.VMEM((CONV_HALO + ts, tc), BF16)],
        compiler_params=_params("parallel", "parallel", "arbitrary"),
    )(xbc, conv_w.astype(F32), conv_b.reshape(1, width).astype(F32))


def _split2(x):
    hi = x.astype(BF16)
    lo = (x - hi.astype(F32)).astype(BF16)
    return hi, lo


def _split3(x):
    hi = x.astype(BF16)
    r = x - hi.astype(F32)
    mid = r.astype(BF16)
    lo = (r - mid.astype(F32)).astype(BF16)
    return hi, mid, lo


def _sb_attn_kernel(q_ref, k_ref, v_ref, o_ref, qs_ref, oacc_ref, run_ref, *, bq, bk, scale):
    i = pl.program_id(2)
    dh = SB_HEAD_DIM
    nh = q_ref.shape[1] // dh
    nkb = bq // bk
    krow = lax.broadcasted_iota(I32, (bk, bk), 0)
    kcol = lax.broadcasted_iota(I32, (bk, bk), 1)
    later_mat = jnp.where(krow > kcol, 1.0, 0.0).astype(BF16)
    for h in range(nh):
        qs_ref[h] = (q_ref[:, h * dh:(h + 1) * dh].astype(F32) * (scale * LOG2E)).astype(BF16)

    def tile(h, start, run, mask, r0=0):
        ks = k_ref[pl.ds(start, bk), h * dh:(h + 1) * dh]
        vs = v_ref[pl.ds(start, bk), h * dh:(h + 1) * dh]
        z = lax.dot_general(qs_ref[h, r0:, :], ks, (((1,), (1,)), ((), ())), preferred_element_type=F32)
        z = z.astype(BF16)
        neg = jnp.minimum(z, 0.0)
        pos = z - neg
        ls = neg - jnp.log(1.0 + jnp.exp2(neg - pos)) * LOG2E
        lnb = ls - z
        if mask is not None:
            lnb = jnp.where(mask, lnb.astype(F32), 0.0).astype(BF16)
        later = jnp.dot(lnb, later_mat, preferred_element_type=F32)
        a = jnp.exp2(ls.astype(F32) + later + jnp.concatenate([run] * (bk // LANES), axis=1))
        if mask is not None:
            a = jnp.where(mask, a, 0.0)
        o = jnp.dot(a.astype(BF16), vs, preferred_element_type=F32)
        total = later[:, 0:1] + lnb[:, 0:1].astype(F32)
        return o, run + jnp.broadcast_to(total, run.shape)

    oacc_ref[...] = jnp.zeros(oacc_ref.shape, F32)
    run_ref[...] = jnp.zeros(run_ref.shape, F32)
    for d in range(nkb):
        r0 = (nkb - 1 - d) * bk
        start = pl.multiple_of(i * bq + r0, bk)
        mask = (lax.broadcasted_iota(I32, (bq - r0, bk), 1) < lax.broadcasted_iota(I32, (bq - r0, bk), 0))
        for h in range(nh):
            o, run = tile(h, start, run_ref[h, r0:, :], mask, r0)
            oacc_ref[h, r0:, :] += o
            run_ref[h, r0:, :] = run
    state = [(oacc_ref[h], run_ref[h]) for h in range(nh)]

    def body(jj, carry):
        out = list(carry)
        for d in range(nkb):
            start = pl.multiple_of(((i - jj) * nkb - 1 - d) * bk, bk)
            for h in range(nh):
                o, run = tile(h, start, out[h][1], None)
                out[h] = (out[h][0] + o, run)
        return tuple(out)

    res = lax.fori_loop(0, i, body, tuple(state))
    o_ref[...] = jnp.concatenate([r[0] for r in res], axis=1).astype(o_ref.dtype)


SB_HEADS_PER_STEP = 2
SB_Q_BLOCK = 1024
SB_K_BLOCK = 256


def _sb_attention(u_b, bsz, seq):
    t, w3 = u_b.shape
    width = w3 // 3
    hw = SB_HEADS_PER_STEP * SB_HEAD_DIM
    groups = width // hw
    bq = _tile(seq, SB_Q_BLOCK)
    bk = min(SB_K_BLOCK, bq)
    assert bq % bk == 0
    nq = seq // bq
    return pl.pallas_call(
        functools.partial(_sb_attn_kernel, bq=bq, bk=bk, scale=SB_HEAD_DIM ** -0.5),
        grid=(bsz, groups, nq),
        in_specs=[pl.BlockSpec((bq, hw), lambda b, h, i: (b * nq + i, h)),
                  pl.BlockSpec((seq, hw), lambda b, h, i: (b, groups + h)),
                  pl.BlockSpec((seq, hw), lambda b, h, i: (b, 2 * groups + h))],
        out_specs=pl.BlockSpec((bq, hw), lambda b, h, i: (b * nq + i, h)),
        out_shape=jax.ShapeDtypeStruct((t, width), BF16),
        scratch_shapes=[pltpu.VMEM((SB_HEADS_PER_STEP, bq, SB_HEAD_DIM), BF16),
                        pltpu.VMEM((SB_HEADS_PER_STEP, bq, SB_HEAD_DIM), F32),
                        pltpu.VMEM((SB_HEADS_PER_STEP, bq, LANES), F32)],
        name="sb_attention",
        compiler_params=_params("parallel", "parallel", "arbitrary"),
    )(u_b, u_b, u_b)


def _ssd_kernel(xs_ref, bm_ref, cm_ref, z_ref, dt_ref, dtb_ref, alog_ref, dsk_ref, nw_ref,
                o_ref, state_ref, acst_ref, *, hpg, hd, lc):
    g = pl.program_id(1)
    gw = hpg * hd
    nsub = xs_ref.shape[0] // lc

    @pl.when(pl.program_id(2) == 0)
    def _():
        state_ref[...] = jnp.zeros(state_ref.shape, F32)

    row = lax.broadcasted_iota(I32, (lc, lc), 0)
    col = lax.broadcasted_iota(I32, (lc, lc), 1)
    causal = row >= col
    tri = jnp.where(causal, 1.0, 0.0).astype(BF16)
    hrow = lax.broadcasted_iota(I32, (LANES, gw), 0)
    hcol = lax.broadcasted_iota(I32, (LANES, gw), 1)
    spread = jnp.where(hrow == g * hpg + (hcol >> _log2(hd)), 1.0, 0.0).astype(BF16)
    hrow2 = lax.broadcasted_iota(I32, (LANES, hpg * lc), 0)
    hcol2 = lax.broadcasted_iota(I32, (LANES, hpg * lc), 1)
    spread_l = jnp.where(hrow2 == g * hpg + (hcol2 >> _log2(lc)), 1.0, 0.0).astype(BF16)

    def expand(x, parts, mat):
        pieces = (_split3(x) if parts == 3 else _split2(x) if parts == 2 else (x.astype(BF16),))
        return sum(jnp.dot(p, mat, preferred_element_type=F32) for p in pieces)

    neg_a = -jnp.exp(alog_ref[...])
    dsk_x = expand(jnp.broadcast_to(dsk_ref[...], (SUBLANES, LANES)), 3, spread)[0:1, :]

    def chunk(sc, prev):
        rows = slice(sc * lc, (sc + 1) * lc)
        dt = jax.nn.softplus(dt_ref[rows, :] + dtb_ref[...])
        a_cs = sum(jnp.dot(tri, part, preferred_element_type=F32) for part in _split3(dt * neg_a))
        acst_ref[sc] = a_cs.T
        exp_acs = jnp.exp(a_cs)
        dte = jnp.exp(a_cs[lc - 1:lc, :] - a_cs)

        dt_x = expand(dt, 1, spread)
        dte_x = expand(dte, 1, spread)
        eacs_x = expand(exp_acs, 2, spread)
        acs_col = expand(a_cs, 2, spread_l)

        xs = xs_ref[rows, :].astype(F32)
        bm = bm_ref[rows, :]
        cm = cm_ref[rows, :]
        xdt = xs * dt_x
        xdt_b = xdt.astype(BF16)
        cb = lax.dot_general(cm, bm, (((1,), (1,)), ((), ())), preferred_element_type=F32)

        y_parts = []
        for r in range(hpg):
            a_l = acs_col[:, r * lc:(r + 1) * lc]
            a_s = acst_ref[sc, pl.ds(g * hpg + r, 1), :]
            seg = jnp.where(causal, jnp.exp(jnp.where(causal, a_l - a_s, 0.0)), 0.0)
            m = (cb * seg).astype(BF16)
            y_parts.append(jnp.dot(m, xdt_b[:, r * hd:(r + 1) * hd], preferred_element_type=F32))
        y_diag = jnp.concatenate(y_parts, axis=1)

        y_off = jnp.dot(cm, prev.astype(BF16), preferred_element_type=F32) * eacs_x
        y = y_diag + y_off + xs * dsk_x

        bm_t = bm.astype(F32).T.astype(BF16)
        new_state = jnp.dot(bm_t, (xdt * dte_x).astype(BF16), preferred_element_type=F32)

        zf = z_ref[rows, :].astype(F32)
        yg = y * (zf * jax.nn.sigmoid(zf))
        ms = jnp.mean(yg * yg, axis=-1, keepdims=True)
        o_ref[rows, :] = (yg * lax.rsqrt(ms + EPS) * nw_ref[...]).astype(o_ref.dtype)
        return prev * eacs_x[lc - 1:lc, :] + new_state

    state = state_ref[...]
    for sc in range(nsub):
        state = chunk(sc, state)
    state_ref[...] = state


def _pad_lanes(v):
    return jnp.zeros((1, LANES), F32).at[0, :v.shape[0]].set(v.astype(F32))


def _ssd(xbc_act, z, dt_raw, dt_bias, a_log, d_skip, norm_w, bsz, seq):
    t, width = z.shape
    heads = width // SSD_HEAD_DIM
    hpg = heads // SSD_GROUPS
    gw = hpg * SSD_HEAD_DIM
    nsub = SSD_CHUNKS_PER_STEP if seq % (SSD_CHUNKS_PER_STEP * SSD_CHUNK) == 0 else 1
    lc = nsub * SSD_CHUNK
    nch = seq // lc
    nbs = width // SSD_STATE
    assert gw * SSD_GROUPS == width and SSD_STATE == LANES and heads <= LANES
    row = lambda b, g, c: b * nch + c
    return pl.pallas_call(
        functools.partial(_ssd_kernel, hpg=hpg, hd=SSD_HEAD_DIM, lc=SSD_CHUNK),
        name="ssd",
        grid=(bsz, SSD_GROUPS, nch),
        in_specs=[pl.BlockSpec((lc, gw), lambda b, g, c: (row(b, g, c), g)),
                  pl.BlockSpec((lc, SSD_STATE), lambda b, g, c: (row(b, g, c), nbs + g)),
                  pl.BlockSpec((lc, SSD_STATE), lambda b, g, c: (row(b, g, c), nbs + SSD_GROUPS + g)),
                  pl.BlockSpec((lc, gw), lambda b, g, c: (row(b, g, c), g)),
                  pl.BlockSpec((lc, LANES), lambda b, g, c: (row(b, g, c), 0)),
                  pl.BlockSpec((1, LANES), lambda b, g, c: (0, 0)),
                  pl.BlockSpec((1, LANES), lambda b, g, c: (0, 0)),
                  pl.BlockSpec((1, LANES), lambda b, g, c: (0, 0)),
                  pl.BlockSpec((1, gw), lambda b, g, c: (0, g))],
        out_specs=pl.BlockSpec((lc, gw), lambda b, g, c: (row(b, g, c), g)),
        out_shape=jax.ShapeDtypeStruct((t, width), BF16),
        scratch_shapes=[pltpu.VMEM((SSD_STATE, gw), F32), pltpu.VMEM((nsub, LANES, SSD_CHUNK), F32)],
        compiler_params=_params("parallel", "parallel", "arbitrary"),
    )(xbc_act, xbc_act, xbc_act, z, dt_raw, _pad_lanes(dt_bias), _pad_lanes(a_log),
      _pad_lanes(d_skip), norm_w.reshape(1, width).astype(F32))


def _topk_rows(s, k):
    n = s.shape[0]
    iota = lax.broadcasted_iota(I32, s.shape, 0)
    vals, idxs = [], []
    for it in range(k):
        m = jnp.max(s, axis=0, keepdims=True)
        i = jnp.min(jnp.where(s == m, iota, n), axis=0, keepdims=True)
        vals.append(m)
        idxs.append(i)
        if it + 1 < k:
            s = jnp.where(iota == i, -jnp.inf, s)
    return jnp.concatenate(vals, axis=0), jnp.concatenate(idxs, axis=0)


def _peer_topk_kernel(q_ref, keys_ref, i1_ref, i2_ref, g_ref, *, topk):
    half = q_ref.shape[1] // 2
    sub_v, sub_i = [], []
    for c in range(2):
        s = lax.dot_general(keys_ref[c], q_ref[:, c * half:(c + 1) * half],
                            (((1,), (1,)), ((), ())), preferred_element_type=F32)
        v, i = _topk_rows(s, topk)
        sub_v.append(v)
        sub_i.append(i)
    n_wide = topk // 2
    pieces, starts = [], []
    for a in range(n_wide):
        nb = min(topk, -(-(topk // (a + 1)) // SUBLANES) * SUBLANES)
        starts.append(sum(p.shape[0] for p in pieces))
        pieces.append(sub_v[0][a:a + 1, :] + sub_v[1][0:nb, :])
    tail_start = sum(p.shape[0] for p in pieces)
    pieces.append(sub_v[0][n_wide:topk, :] + sub_v[1][0:1, :])
    top_s, pos = _topk_rows(jnp.concatenate(pieces, axis=0), topk)
    pa = jnp.zeros(pos.shape, I32)
    pb = pos
    for a in range(1, n_wide):
        ge = pos >= starts[a]
        pa = jnp.where(ge, a, pa)
        pb = jnp.where(ge, pos - starts[a], pb)
    ge = pos >= tail_start
    pa = jnp.where(ge, pos + (n_wide - tail_start), pa)
    pb = jnp.where(ge, 0, pb)
    i1 = jnp.zeros(pos.shape, I32)
    i2 = jnp.zeros(pos.shape, I32)
    for a in range(topk):
        i1 = jnp.where(pa == a, sub_i[0][a:a + 1, :], i1)
        i2 = jnp.where(pb == a, sub_i[1][a:a + 1, :], i2)
    e = jnp.exp(top_s - top_s[0:1, :])
    i1_ref[...] = i1
    i2_ref[...] = i2
    g_ref[...] = e / jnp.sum(e, axis=0, keepdims=True)


def _peer_topk(q, sub_keys):
    t, qw = q.shape
    heads, _, nkeys, dk2 = sub_keys.shape
    tt = _tile(t, 512)
    hk = heads * PEER_TOPK
    out = jax.ShapeDtypeStruct((hk, t), I32)
    spec = pl.BlockSpec((PEER_TOPK, tt), lambda i, h: (h, i))
    return pl.pallas_call(
        functools.partial(_peer_topk_kernel, topk=PEER_TOPK),
        name="peer_topk",
        grid=(t // tt, heads),
        in_specs=[pl.BlockSpec((tt, 2 * dk2), lambda i, h: (i, h)),
                  pl.BlockSpec((None, 2, nkeys, dk2), lambda i, h: (h, 0, 0, 0))],
        out_specs=[spec, spec, spec],
        out_shape=[out, out, jax.ShapeDtypeStruct((hk, t), F32)],
        compiler_params=_params("parallel", "parallel"),
    )(q, sub_keys.astype(BF16))


def _peer_gates_kernel(i1_ref, i2_ref, g_ref, o_ref, i1s_ref, i2s_ref, gs_ref):
    nk = o_ref.shape[2]
    i1s_ref[...] = i1_ref[...].astype(F32).T
    i2s_ref[...] = i2_ref[...].astype(F32).T
    gs_ref[...] = g_ref[...].T
    key = lax.broadcasted_iota(I32, (nk, i1_ref.shape[0]), 0).astype(F32)

    def body(t, _):
        r1 = i1s_ref[pl.ds(t, 1), :]
        r2 = i2s_ref[pl.ds(t, 1), :]
        gg = gs_ref[pl.ds(t, 1), :]
        a_mat = jnp.where(key == r1, gg, 0.0).astype(BF16)
        b_mat = jnp.where(key == r2, 1.0, 0.0).astype(BF16)
        gmap = lax.dot_general(a_mat, b_mat, (((1,), (1,)), ((), ())), preferred_element_type=F32)
        o_ref[:, pl.ds(pl.multiple_of(t * SUBLANES, SUBLANES), SUBLANES), :] = gmap.reshape(
            nk // SUBLANES, SUBLANES, nk)
        return 0

    lax.fori_loop(0, i1s_ref.shape[0], body, 0, unroll=GATES_UNROLL)


def _peer_gates(i1t, i2t, gt, nkeys):
    hk, t = i1t.shape
    tt = _tile(t, 128)
    spec = pl.BlockSpec((hk, tt), lambda i: (0, i))
    return pl.pallas_call(
        _peer_gates_kernel,
        name="peer_gates",
        grid=(t // tt,),
        in_specs=[spec, spec, spec],
        out_specs=pl.BlockSpec((nkeys // SUBLANES, tt * SUBLANES, nkeys), lambda i: (0, i, 0)),
        out_shape=jax.ShapeDtypeStruct((nkeys // SUBLANES, t * SUBLANES, nkeys), F32),
        scratch_shapes=[pltpu.VMEM((tt, hk), F32)] * 3,
        compiler_params=_params("parallel"),
    )(i1t, i2t, gt)


def _peer_ffn_kernel(x_ref, dn_ref, up_ref, gd_ref, o_ref):
    j = pl.program_id(1)
    tm = x_ref.shape[0]
    nk = gd_ref.shape[1]
    per = SUBLANES // 2
    half = o_ref.shape[1] // 2

    @pl.when(j == 0)
    def _():
        o_ref[...] = jnp.zeros(o_ref.shape, F32)

    def step(part):
        s = jnp.dot(x_ref[...], dn_ref[...], preferred_element_type=F32)
        pieces = []
        for c in range(per):
            sc = s[:, c * nk:(c + 1) * nk]
            act = 0.5 * sc * (1.0 + lax.erf(sc * (2.0 ** -0.5)))
            gate = gd_ref[pl.ds(part * per + c, tm, stride=SUBLANES), :]
            pieces.append((act * gate).astype(BF16))
        a = jnp.concatenate(pieces, axis=1)
        for n in range(2):
            cols = slice(n * half, (n + 1) * half)
            o_ref[:, cols] += jnp.dot(a, up_ref[:, cols], preferred_element_type=F32)

    for part in range(2):
        pl.when(lax.rem(j, 2) == part)(functools.partial(step, part))


def _peer_ffn(n2, down, up, gd3):
    t, d = n2.shape
    ne = up.shape[0]
    nk = gd3.shape[2]
    tm = _tile(t, 1024)
    te = SUBLANES * nk // 2
    nj = ne // te
    assert nj % 2 == 0
    down_t = down.reshape(nj, te, d).transpose(0, 2, 1)
    return pl.pallas_call(
        _peer_ffn_kernel,
        name="peer_ffn",
        grid=(t // tm, nj),
        in_specs=[pl.BlockSpec((tm, d), lambda i, j: (i, 0)),
                  pl.BlockSpec((None, d, te), lambda i, j: (j, 0, 0)),
                  pl.BlockSpec((te, d), lambda i, j: (j, 0)),
                  pl.BlockSpec((None, tm * SUBLANES, nk), lambda i, j: (j // 2, i, 0))],
        out_specs=pl.BlockSpec((tm, d), lambda i, j: (i, 0)),
        out_shape=jax.ShapeDtypeStruct((t, d), F32),
        compiler_params=_params("parallel", "arbitrary", vmem=VMEM_LIMIT_FFN),
    )(n2, down_t, up, gd3)


def _hybrid_mixer(h, n, l, bsz, seq, w_in, conv_a_w, ssd_conv_w, ssd_conv_b, ssd_dt_bias,
                  ssd_a_log, ssd_d, ssd_norm_w, w_branch_a, w_branch_b, w_branch_c, w_out):
    d = h.shape[1]
    conv_w = conv_a_w.shape[2]
    sb_w = w_branch_b.shape[1]
    ssd_w = w_branch_c.shape[1]
    xbc_w = ssd_conv_w.shape[2]
    heads = ssd_d.shape[1]
    bounds = [0, 3 * conv_w, 3 * sb_w, ssd_w, xbc_w, heads, 3 * d]
    offs = [sum(bounds[:k + 1]) for k in range(len(bounds))]
    w = w_in[l]
    w_a, w_b, w_z, w_xbc, w_dt, w_g = (w[:, offs[k]:offs[k + 1]].astype(BF16) for k in range(6))
    w_dt = jnp.pad(w_dt, ((0, 0), (0, LANES - heads)))

    u_a = _matmul(n, w_a, BF16)
    u_b = _matmul(n, w_b, BF16)
    u_z = _matmul(n, w_z, BF16)
    u_xbc = _matmul(n, w_xbc, BF16)
    u_dt = _matmul(n, w_dt, F32)
    u_g = _matmul(n, w_g, BF16)

    mix_a = _short_conv(u_a, conv_a_w[l], bsz, seq)
    mix_b = _sb_attention(u_b, bsz, seq)
    xbc_act = _ssd_conv(u_xbc, ssd_conv_w[l], ssd_conv_b[l], bsz, seq)
    mix_c = _ssd(xbc_act, u_z, u_dt, ssd_dt_bias[l], ssd_a_log[l], ssd_d[l], ssd_norm_w[l], bsz, seq)

    m = _matmul(mix_a, w_branch_a[l].astype(BF16), F32, gate=u_g)
    m = _matmul(mix_b, w_branch_b[l].astype(BF16), F32, gate=u_g, gate_col=d, add=m)
    m = _matmul(mix_c, w_branch_c[l].astype(BF16), BF16, gate=u_g, gate_col=2 * d, add=m, tm=512)
    return _matmul(m, w_out[l].astype(BF16), F32, add=h)


def _peer(h, l, norm_ffn_w, peer_w_query, peer_sub_keys, peer_down, peer_up):
    n2 = _rmsnorm(h, norm_ffn_w[l], BF16)
    q = _matmul(n2, peer_w_query[l].astype(BF16), BF16)
    i1t, i2t, gt = _peer_topk(q, peer_sub_keys[l])
    gd3 = _peer_gates(i1t, i2t, gt, peer_sub_keys.shape[3])
    return _peer_ffn(n2, peer_down[l].astype(BF16), peer_up[l].astype(BF16), gd3)


def kernel(x, norm_mix_w, w_in, conv_a_w, ssd_conv_w, ssd_conv_b, ssd_dt_bias, ssd_a_log, ssd_d,
           ssd_norm_w, w_branch_a, w_branch_b, w_branch_c, w_out, norm_ffn_w, peer_w_query,
           peer_sub_keys, peer_down, peer_up, final_norm_w):
    bsz, seq, d = x.shape
    h = x.reshape(bsz * seq, d)
    ffn = None
    for l in range(w_in.shape[0]):
        if ffn is None:
            n = _rmsnorm(h, norm_mix_w[l], BF16)
        else:
            h, n = _rmsnorm(h, norm_mix_w[l], BF16, add=ffn)
        h = _hybrid_mixer(h, n, l, bsz, seq, w_in, conv_a_w, ssd_conv_w, ssd_conv_b,
                          ssd_dt_bias, ssd_a_log, ssd_d, ssd_norm_w, w_branch_a, w_branch_b,
                          w_branch_c, w_out)
        ffn = _peer(h, l, norm_ffn_w, peer_w_query, peer_sub_keys, peer_down, peer_up)
    return _rmsnorm(h, final_norm_w, x.dtype, add=ffn, keep_sum=False).reshape(bsz, seq, d)
```

```python
import functools

import jax
import jax.numpy as jnp
from jax import lax
from jax.experimental import pallas as pl
from jax.experimental.pallas import tpu as pltpu

F32 = jnp.float32
BF16 = jnp.bfloat16
I32 = jnp.int32

EPS = 1e-6
LOG2E = 1.4426950408889634
SB_HEAD_DIM = 128
SSD_HEAD_DIM = 64
SSD_GROUPS = 8
SSD_STATE = 128
SSD_CHUNK = 128
SSD_CHUNKS_PER_STEP = 4
GATES_UNROLL = 64
PEER_HEADS = 8
PEER_TOPK = 16

LANES = 128
SUBLANES = 8
VMEM_LIMIT = 48 * 1024 * 1024
VMEM_LIMIT_FFN = 56 * 1024 * 1024


def _params(*sem, vmem=VMEM_LIMIT):
    return pltpu.CompilerParams(dimension_semantics=sem, vmem_limit_bytes=vmem)


def _log2(n):
    assert n & (n - 1) == 0
    return n.bit_length() - 1


def _tile(n, pref):
    if n <= pref:
        return n
    t = pref
    while n % t:
        t -= SUBLANES
    return t


def _rmsnorm_kernel(x_ref, w_ref, o_ref):
    x = x_ref[...].astype(F32)
    ms = jnp.mean(x * x, axis=-1, keepdims=True)
    o_ref[...] = (x * lax.rsqrt(ms + EPS) * w_ref[...]).astype(o_ref.dtype)


def _add_rmsnorm_kernel(x_ref, a_ref, w_ref, *out_refs):
    h = x_ref[...] + a_ref[...]
    if len(out_refs) == 2:
        out_refs[0][...] = h
    ms = jnp.mean(h * h, axis=-1, keepdims=True)
    out_refs[-1][...] = (h * lax.rsqrt(ms + EPS) * w_ref[...]).astype(out_refs[-1].dtype)


def _rmsnorm(x, w, out_dtype, add=None, keep_sum=True):
    t, d = x.shape
    tm = _tile(t, 512)
    row = pl.BlockSpec((tm, d), lambda i: (i, 0))
    wspec = pl.BlockSpec((1, d), lambda i: (0, 0))
    normed = jax.ShapeDtypeStruct((t, d), out_dtype)
    w2 = w.reshape(1, d).astype(F32)
    if add is None:
        return pl.pallas_call(
            _rmsnorm_kernel,
            name="rmsnorm",
            grid=(t // tm,),
            in_specs=[row, wspec],
            out_specs=row,
            out_shape=normed,
            compiler_params=_params("parallel"),
        )(x, w2)
    return pl.pallas_call(
        _add_rmsnorm_kernel,
        name="add_rmsnorm",
        grid=(t // tm,),
        in_specs=[row, row, wspec],
        out_specs=[row, row] if keep_sum else row,
        out_shape=[jax.ShapeDtypeStruct((t, d), F32), normed] if keep_sum else normed,
        compiler_params=_params("parallel"),
    )(x, add, w2)


def _mm_kernel(*refs, has_gate, has_add):
    x_ref, w_ref = refs[0], refs[1]
    o_ref = refs[-1]
    acc = jnp.dot(x_ref[...], w_ref[...], preferred_element_type=F32)
    k = 2
    if has_gate:
        acc = acc * jax.nn.sigmoid(refs[k][...].astype(F32))
        k += 1
    if has_add:
        acc = acc + refs[k][...].astype(F32)
    o_ref[...] = acc.astype(o_ref.dtype)


def _matmul(x, w, out_dtype, gate=None, gate_col=0, add=None, tm=1024, tn=1024, name="matmul"):
    m, kd = x.shape
    n = w.shape[1]
    tm = _tile(m, tm)
    tn = min(tn, n)
    assert n % tn == 0 and gate_col % tn == 0
    goff = gate_col // tn
    ins = [x, w]
    specs = [pl.BlockSpec((tm, kd), lambda i, j: (i, 0)),
             pl.BlockSpec((kd, tn), lambda i, j: (0, j))]
    if gate is not None:
        ins.append(gate)
        specs.append(pl.BlockSpec((tm, tn), lambda i, j: (i, goff + j)))
    if add is not None:
        ins.append(add)
        specs.append(pl.BlockSpec((tm, tn), lambda i, j: (i, j)))
    return pl.pallas_call(
        functools.partial(_mm_kernel, has_gate=gate is not None, has_add=add is not None),
        name=name,
        grid=(m // tm, n // tn),
        in_specs=specs,
        out_specs=pl.BlockSpec((tm, tn), lambda i, j: (i, j)),
        out_shape=jax.ShapeDtypeStruct((m, n), out_dtype),
        compiler_params=_params("parallel", "parallel"),
    )(*ins)


CONV_SEQ_TILE = 256
CONV_COL_TILE = 2048


def _conv_taps(cur, buf_ref, w_ref, first):
    ts = cur.shape[0]
    kw = w_ref.shape[0]

    @pl.when(first)
    def _():
        buf_ref[0:SUBLANES, :] = jnp.zeros((SUBLANES, cur.shape[1]), F32)

    buf_ref[SUBLANES:SUBLANES + ts, :] = cur
    y = cur * w_ref[kw - 1:kw, :]
    for d in range(1, kw):
        y = y + buf_ref[SUBLANES - d:SUBLANES - d + ts, :] * w_ref[kw - 1 - d:kw - d, :]
    buf_ref[0:SUBLANES, :] = cur[ts - SUBLANES:ts, :]
    return y


def _short_conv_kernel(b_ref, c_ref, h_ref, w_ref, o_ref, buf_ref):
    cur = c_ref[...].astype(F32) * h_ref[...].astype(F32)
    y = _conv_taps(cur, buf_ref, w_ref, pl.program_id(2) == 0)
    o_ref[...] = (b_ref[...].astype(F32) * y).astype(o_ref.dtype)


def _short_conv(u_a, conv_w, bsz, seq):
    t = u_a.shape[0]
    width = conv_w.shape[1]
    ts = _tile(seq, CONV_SEQ_TILE)
    tc = min(CONV_COL_TILE, width)
    nc = width // tc
    ns = seq // ts
    row = lambda b, j, s: b * ns + s
    return pl.pallas_call(
        _short_conv_kernel,
        name="short_conv",
        grid=(bsz, nc, ns),
        in_specs=[pl.BlockSpec((ts, tc), lambda b, j, s: (row(b, j, s), j)),
                  pl.BlockSpec((ts, tc), lambda b, j, s: (row(b, j, s), nc + j)),
                  pl.BlockSpec((ts, tc), lambda b, j, s: (row(b, j, s), 2 * nc + j)),
                  pl.BlockSpec((conv_w.shape[0], tc), lambda b, j, s: (0, j))],
        out_specs=pl.BlockSpec((ts, tc), lambda b, j, s: (row(b, j, s), j)),
        out_shape=jax.ShapeDtypeStruct((t, width), BF16),
        scratch_shapes=[pltpu.VMEM((SUBLANES + ts, tc), F32)],
        compiler_params=_params("parallel", "parallel", "arbitrary"),
    )(u_a, u_a, u_a, conv_w.astype(F32))


CONV_ROWS = 128
CONV_HALO = 16


def _ssd_conv_kernel(x_ref, w_ref, b_ref, o_ref, buf_ref):
    ts, tc = x_ref.shape
    kw = w_ref.shape[0]

    @pl.when(pl.program_id(2) == 0)
    def _():
        buf_ref[0:CONV_HALO, :] = jnp.zeros((CONV_HALO, tc), BF16)

    buf_ref[CONV_HALO:CONV_HALO + ts, :] = x_ref[...]
    row = lax.broadcasted_iota(I32, (CONV_ROWS, CONV_ROWS + CONV_HALO), 0)
    col = lax.broadcasted_iota(I32, (CONV_ROWS, CONV_ROWS + CONV_HALO), 1)
    shift = [jnp.where(col == row + (CONV_HALO - d), 1.0, 0.0).astype(BF16) for d in range(1, kw)]
    outs = []
    for r in range(ts // CONV_ROWS):
        ext = buf_ref[r * CONV_ROWS:(r + 1) * CONV_ROWS + CONV_HALO, :]
        y = ext[CONV_HALO:, :].astype(F32) * w_ref[kw - 1:kw, :] + b_ref[...]
        for d in range(1, kw):
            y = y + jnp.dot(shift[d - 1], ext, preferred_element_type=F32) * w_ref[kw - 1 - d:kw - d, :]
        outs.append((y * jax.nn.sigmoid(y)).astype(o_ref.dtype))
    o_ref[...] = jnp.concatenate(outs, axis=0)
    buf_ref[0:CONV_HALO, :] = x_ref[ts - CONV_HALO:ts, :]


def _ssd_conv(xbc, conv_w, conv_b, bsz, seq):
    t, width = xbc.shape
    ts = _tile(seq, CONV_SEQ_TILE)
    tc = min(CONV_COL_TILE, width)
    nc = width // tc
    ns = seq // ts
    return pl.pallas_call(
        _ssd_conv_kernel,
        name="ssd_conv",
        grid=(bsz, nc, ns),
        in_specs=[pl.BlockSpec((ts, tc), lambda b, j, s: (b * ns + s, j)),
                  pl.BlockSpec((conv_w.shape[0], tc), lambda b, j, s: (0, j)),
                  pl.BlockSpec((1, tc), lambda b, j, s: (0, j))],
        out_specs=pl.BlockSpec((ts, tc), lambda b, j, s: (b * ns + s, j)),
        out_shape=jax.ShapeDtypeStruct((t, width), BF16),
        scratch_shapes=[pltpu.VMEM((CONV_HALO + ts, tc), BF16)],
        compiler_params=_params("parallel", "parallel", "arbitrary"),
    )(xbc, conv_w.astype(F32), conv_b.reshape(1, width).astype(F32))


def _split2(x):
    hi = x.astype(BF16)
    lo = (x - hi.astype(F32)).astype(BF16)
    return hi, lo


def _split3(x):
    hi = x.astype(BF16)
    r = x - hi.astype(F32)
    mid = r.astype(BF16)
    lo = (r - mid.astype(F32)).astype(BF16)
    return hi, mid, lo


def _sb_attn_kernel(q_ref, k_ref, v_ref, o_ref, qs_ref, oacc_ref, run_ref, *, bq, bk, scale):
    i = pl.program_id(2)
    dh = SB_HEAD_DIM
    nh = q_ref.shape[1] // dh
    nkb = bq // bk
    krow = lax.broadcasted_iota(I32, (bk, bk), 0)
    kcol = lax.broadcasted_iota(I32, (bk, bk), 1)
    later_mat = jnp.where(krow > kcol, 1.0, 0.0).astype(BF16)
    for h in range(nh):
        qs_ref[h] = (q_ref[:, h * dh:(h + 1) * dh].astype(F32) * (scale * LOG2E)).astype(BF16)

    def tile(h, start, run, mask, r0=0):
        ks = k_ref[pl.ds(start, bk), h * dh:(h + 1) * dh]
        vs = v_ref[pl.ds(start, bk), h * dh:(h + 1) * dh]
        z = lax.dot_general(qs_ref[h, r0:, :], ks, (((1,), (1,)), ((), ())), preferred_element_type=F32)
        z = z.astype(BF16)
        neg = jnp.minimum(z, 0.0)
        pos = z - neg
        ls = neg - jnp.log(1.0 + jnp.exp2(neg - pos)) * LOG2E
        lnb = ls - z
        if mask is not None:
            lnb = jnp.where(mask, lnb.astype(F32), 0.0).astype(BF16)
        later = jnp.dot(lnb, later_mat, preferred_element_type=F32)
        a = jnp.exp2(ls.astype(F32) + later + jnp.concatenate([run] * (bk // LANES), axis=1))
        if mask is not None:
            a = jnp.where(mask, a, 0.0)
        o = jnp.dot(a.astype(BF16), vs, preferred_element_type=F32)
        total = later[:, 0:1] + lnb[:, 0:1].astype(F32)
        return o, run + jnp.broadcast_to(total, run.shape)

    oacc_ref[...] = jnp.zeros(oacc_ref.shape, F32)
    run_ref[...] = jnp.zeros(run_ref.shape, F32)
    for d in range(nkb):
        r0 = (nkb - 1 - d) * bk
        start = pl.multiple_of(i * bq + r0, bk)
        mask = (lax.broadcasted_iota(I32, (bq - r0, bk), 1) < lax.broadcasted_iota(I32, (bq - r0, bk), 0))
        for h in range(nh):
            o, run = tile(h, start, run_ref[h, r0:, :], mask, r0)
            oacc_ref[h, r0:, :] += o
            run_ref[h, r0:, :] = run
    state = [(oacc_ref[h], run_ref[h]) for h in range(nh)]

    def body(jj, carry):
        out = list(carry)
        for d in range(nkb):
            start = pl.multiple_of(((i - jj) * nkb - 1 - d) * bk, bk)
            for h in range(nh):
                o, run = tile(h, start, out[h][1], None)
                out[h] = (out[h][0] + o, run)
        return tuple(out)

    res = lax.fori_loop(0, i, body, tuple(state))
    o_ref[...] = jnp.concatenate([r[0] for r in res], axis=1).astype(o_ref.dtype)


SB_HEADS_PER_STEP = 2
SB_Q_BLOCK = 1024
SB_K_BLOCK = 256


def _sb_attention(u_b, bsz, seq):
    t, w3 = u_b.shape
    width = w3 // 3
    hw = SB_HEADS_PER_STEP * SB_HEAD_DIM
    groups = width // hw
    bq = _tile(seq, SB_Q_BLOCK)
    bk = min(SB_K_BLOCK, bq)
    assert bq % bk == 0
    nq = seq // bq
    return pl.pallas_call(
        functools.partial(_sb_attn_kernel, bq=bq, bk=bk, scale=SB_HEAD_DIM ** -0.5),
        grid=(bsz, groups, nq),
        in_specs=[pl.BlockSpec((bq, hw), lambda b, h, i: (b * nq + i, h)),
                  pl.BlockSpec((seq, hw), lambda b, h, i: (b, groups + h)),
                  pl.BlockSpec((seq, hw), lambda b, h, i: (b, 2 * groups + h))],
        out_specs=pl.BlockSpec((bq, hw), lambda b, h, i: (b * nq + i, h)),
        out_shape=jax.ShapeDtypeStruct((t, width), BF16),
        scratch_shapes=[pltpu.VMEM((SB_HEADS_PER_STEP, bq, SB_HEAD_DIM), BF16),
                        pltpu.VMEM((SB_HEADS_PER_STEP, bq, SB_HEAD_DIM), F32),
                        pltpu.VMEM((SB_HEADS_PER_STEP, bq, LANES), F32)],
        name="sb_attention",
        compiler_params=_params("parallel", "parallel", "arbitrary"),
    )(u_b, u_b, u_b)


def _ssd_kernel(xbc_ref, z_ref, dt_ref, dtb_ref, alog_ref, dsk_ref, nw_ref,
                o_ref, state_ref, acst_ref, *, hpg, hd, lc):
    g = pl.program_id(1)
    gw = hpg * hd
    nsub = xbc_ref.shape[0] // lc
    ns = (xbc_ref.shape[1] - gw) // 2

    @pl.when(pl.program_id(2) == 0)
    def _():
        state_ref[...] = jnp.zeros(state_ref.shape, F32)

    row = lax.broadcasted_iota(I32, (lc, lc), 0)
    col = lax.broadcasted_iota(I32, (lc, lc), 1)
    causal = row >= col
    tri = jnp.where(causal, 1.0, 0.0).astype(BF16)
    hrow = lax.broadcasted_iota(I32, (LANES, gw), 0)
    hcol = lax.broadcasted_iota(I32, (LANES, gw), 1)
    spread = jnp.where(hrow == g * hpg + (hcol >> _log2(hd)), 1.0, 0.0).astype(BF16)
    hrow2 = lax.broadcasted_iota(I32, (LANES, hpg * lc), 0)
    hcol2 = lax.broadcasted_iota(I32, (LANES, hpg * lc), 1)
    spread_l = jnp.where(hrow2 == g * hpg + (hcol2 >> _log2(lc)), 1.0, 0.0).astype(BF16)

    def expand(x, parts, mat):
        pieces = (_split3(x) if parts == 3 else _split2(x) if parts == 2 else (x.astype(BF16),))
        return sum(jnp.dot(p, mat, preferred_element_type=F32) for p in pieces)

    neg_a = -jnp.exp(alog_ref[...])
    dsk_x = expand(jnp.broadcast_to(dsk_ref[...], (SUBLANES, LANES)), 3, spread)[0:1, :]

    def chunk(sc, prev):
        rows = slice(sc * lc, (sc + 1) * lc)
        dt = jax.nn.softplus(dt_ref[rows, :] + dtb_ref[...])
        a_cs = sum(jnp.dot(tri, part, preferred_element_type=F32) for part in _split3(dt * neg_a))
        acst_ref[sc] = a_cs.T
        exp_acs = jnp.exp(a_cs)
        dte = jnp.exp(a_cs[lc - 1:lc, :] - a_cs)

        dt_x = expand(dt, 1, spread)
        dte_x = expand(dte, 1, spread)
        eacs_x = expand(exp_acs, 2, spread)
        acs_col = expand(a_cs, 2, spread_l)

        xs = xbc_ref[rows, 0:gw].astype(F32)
        bm = xbc_ref[rows, gw:gw + ns]
        cm = xbc_ref[rows, gw + ns:gw + 2 * ns]
        xdt = xs * dt_x
        xdt_b = xdt.astype(BF16)
        cb = lax.dot_general(cm, bm, (((1,), (1,)), ((), ())), preferred_element_type=F32)

        y_parts = []
        for r in range(hpg):
            a_l = acs_col[:, r * lc:(r + 1) * lc]
            a_s = acst_ref[sc, pl.ds(g * hpg + r, 1), :]
            seg = jnp.where(causal, jnp.exp(jnp.where(causal, a_l - a_s, 0.0)), 0.0)
            m = (cb * seg).astype(BF16)
            y_parts.append(jnp.dot(m, xdt_b[:, r * hd:(r + 1) * hd], preferred_element_type=F32))
        y_diag = jnp.concatenate(y_parts, axis=1)

        y_off = jnp.dot(cm, prev.astype(BF16), preferred_element_type=F32) * eacs_x
        y = y_diag + y_off + xs * dsk_x

        bm_t = bm.astype(F32).T.astype(BF16)
        new_state = jnp.dot(bm_t, (xdt * dte_x).astype(BF16), preferred_element_type=F32)

        zf = z_ref[rows, :].astype(F32)
        yg = y * (zf * jax.nn.sigmoid(zf))
        ms = jnp.mean(yg * yg, axis=-1, keepdims=True)
        o_ref[rows, :] = (yg * lax.rsqrt(ms + EPS) * nw_ref[...]).astype(o_ref.dtype)
        return prev * eacs_x[lc - 1:lc, :] + new_state

    state = state_ref[...]
    for sc in range(nsub):
        state = chunk(sc, state)
    state_ref[...] = state


def _pad_lanes(v):
    return jnp.zeros((1, LANES), F32).at[0, :v.shape[0]].set(v.astype(F32))


def _ssd_group_major(width):
    gw = width // SSD_GROUPS
    base_b = width
    base_c = width + SSD_GROUPS * SSD_STATE
    cols = []
    for g in range(SSD_GROUPS):
        cols += list(range(g * gw, (g + 1) * gw))
        cols += list(range(base_b + g * SSD_STATE, base_b + (g + 1) * SSD_STATE))
        cols += list(range(base_c + g * SSD_STATE, base_c + (g + 1) * SSD_STATE))
    return jnp.asarray(cols, I32)


def _ssd(xbc_act, z, dt_raw, dt_bias, a_log, d_skip, norm_w, bsz, seq):
    t, width = z.shape
    heads = width // SSD_HEAD_DIM
    hpg = heads // SSD_GROUPS
    gw = hpg * SSD_HEAD_DIM
    nsub = SSD_CHUNKS_PER_STEP if seq % (SSD_CHUNKS_PER_STEP * SSD_CHUNK) == 0 else 1
    lc = nsub * SSD_CHUNK
    nch = seq // lc
    assert gw * SSD_GROUPS == width and SSD_STATE == LANES and heads <= LANES
    row = lambda b, g, c: b * nch + c
    return pl.pallas_call(
        functools.partial(_ssd_kernel, hpg=hpg, hd=SSD_HEAD_DIM, lc=SSD_CHUNK),
        name="ssd",
        grid=(bsz, SSD_GROUPS, nch),
        in_specs=[pl.BlockSpec((lc, gw + 2 * SSD_STATE), lambda b, g, c: (row(b, g, c), g)),
                  pl.BlockSpec((lc, gw), lambda b, g, c: (row(b, g, c), g)),
                  pl.BlockSpec((lc, LANES), lambda b, g, c: (row(b, g, c), 0)),
                  pl.BlockSpec((1, LANES), lambda b, g, c: (0, 0)),
                  pl.BlockSpec((1, LANES), lambda b, g, c: (0, 0)),
                  pl.BlockSpec((1, LANES), lambda b, g, c: (0, 0)),
                  pl.BlockSpec((1, gw), lambda b, g, c: (0, g))],
        out_specs=pl.BlockSpec((lc, gw), lambda b, g, c: (row(b, g, c), g)),
        out_shape=jax.ShapeDtypeStruct((t, width), BF16),
        scratch_shapes=[pltpu.VMEM((SSD_STATE, gw), F32), pltpu.VMEM((nsub, LANES, SSD_CHUNK), F32)],
        compiler_params=_params("parallel", "parallel", "arbitrary"),
    )(xbc_act, z, dt_raw, _pad_lanes(dt_bias), _pad_lanes(a_log),
      _pad_lanes(d_skip), norm_w.reshape(1, width).astype(F32))


def _topk_rows(s, k):
    n = s.shape[0]
    iota = lax.broadcasted_iota(I32, s.shape, 0)
    vals, idxs = [], []
    for it in range(k):
        m = jnp.max(s, axis=0, keepdims=True)
        i = jnp.min(jnp.where(s == m, iota, n), axis=0, keepdims=True)
        vals.append(m)
        idxs.append(i)
        if it + 1 < k:
            s = jnp.where(iota == i, -jnp.inf, s)
    return jnp.concatenate(vals, axis=0), jnp.concatenate(idxs, axis=0)


def _peer_topk_kernel(q_ref, keys_ref, i1_ref, i2_ref, g_ref, *, topk):
    half = q_ref.shape[1] // 2
    sub_v, sub_i = [], []
    for c in range(2):
        s = lax.dot_general(keys_ref[c], q_ref[:, c * half:(c + 1) * half],
                            (((1,), (1,)), ((), ())), preferred_element_type=F32)
        v, i = _topk_rows(s, topk)
        sub_v.append(v)
        sub_i.append(i)
    n_wide = topk // 2
    pieces, starts = [], []
    for a in range(n_wide):
        nb = min(topk, -(-(topk // (a + 1)) // SUBLANES) * SUBLANES)
        starts.append(sum(p.shape[0] for p in pieces))
        pieces.append(sub_v[0][a:a + 1, :] + sub_v[1][0:nb, :])
    tail_start = sum(p.shape[0] for p in pieces)
    pieces.append(sub_v[0][n_wide:topk, :] + sub_v[1][0:1, :])
    top_s, pos = _topk_rows(jnp.concatenate(pieces, axis=0), topk)
    pa = jnp.zeros(pos.shape, I32)
    pb = pos
    for a in range(1, n_wide):
        ge = pos >= starts[a]
        pa = jnp.where(ge, a, pa)
        pb = jnp.where(ge, pos - starts[a], pb)
    ge = pos >= tail_start
    pa = jnp.where(ge, pos + (n_wide - tail_start), pa)
    pb = jnp.where(ge, 0, pb)
    i1 = jnp.zeros(pos.shape, I32)
    i2 = jnp.zeros(pos.shape, I32)
    for a in range(topk):
        i1 = jnp.where(pa == a, sub_i[0][a:a + 1, :], i1)
        i2 = jnp.where(pb == a, sub_i[1][a:a + 1, :], i2)
    e = jnp.exp(top_s - top_s[0:1, :])
    i1_ref[...] = i1
    i2_ref[...] = i2
    g_ref[...] = e / jnp.sum(e, axis=0, keepdims=True)


def _peer_topk(q, sub_keys):
    t, qw = q.shape
    heads, _, nkeys, dk2 = sub_keys.shape
    tt = _tile(t, 512)
    hk = heads * PEER_TOPK
    out = jax.ShapeDtypeStruct((hk, t), I32)
    spec = pl.BlockSpec((PEER_TOPK, tt), lambda i, h: (h, i))
    return pl.pallas_call(
        functools.partial(_peer_topk_kernel, topk=PEER_TOPK),
        name="peer_topk",
        grid=(t // tt, heads),
        in_specs=[pl.BlockSpec((tt, 2 * dk2), lambda i, h: (i, h)),
                  pl.BlockSpec((None, 2, nkeys, dk2), lambda i, h: (h, 0, 0, 0))],
        out_specs=[spec, spec, spec],
        out_shape=[out, out, jax.ShapeDtypeStruct((hk, t), F32)],
        compiler_params=_params("parallel", "parallel"),
    )(q, sub_keys.astype(BF16))


def _peer_gates_kernel(i1_ref, i2_ref, g_ref, o_ref, i1s_ref, i2s_ref, gs_ref):
    nk = o_ref.shape[2]
    i1s_ref[...] = i1_ref[...].astype(F32).T
    i2s_ref[...] = i2_ref[...].astype(F32).T
    gs_ref[...] = g_ref[...].T
    key = lax.broadcasted_iota(I32, (nk, i1_ref.shape[0]), 0).astype(F32)

    def body(t, _):
        r1 = i1s_ref[pl.ds(t, 1), :]
        r2 = i2s_ref[pl.ds(t, 1), :]
        gg = gs_ref[pl.ds(t, 1), :]
        a_mat = jnp.where(key == r1, gg, 0.0).astype(BF16)
        b_mat = jnp.where(key == r2, 1.0, 0.0).astype(BF16)
        gmap = lax.dot_general(a_mat, b_mat, (((1,), (1,)), ((), ())), preferred_element_type=F32)
        o_ref[:, pl.ds(pl.multiple_of(t * SUBLANES, SUBLANES), SUBLANES), :] = gmap.reshape(
            nk // SUBLANES, SUBLANES, nk)
        return 0

    lax.fori_loop(0, i1s_ref.shape[0], body, 0, unroll=GATES_UNROLL)


def _peer_gates(i1t, i2t, gt, nkeys):
    hk, t = i1t.shape
    tt = _tile(t, 128)
    spec = pl.BlockSpec((hk, tt), lambda i: (0, i))
    return pl.pallas_call(
        _peer_gates_kernel,
        name="peer_gates",
        grid=(t // tt,),
        in_specs=[spec, spec, spec],
        out_specs=pl.BlockSpec((nkeys // SUBLANES, tt * SUBLANES, nkeys), lambda i: (0, i, 0)),
        out_shape=jax.ShapeDtypeStruct((nkeys // SUBLANES, t * SUBLANES, nkeys), F32),
        scratch_shapes=[pltpu.VMEM((tt, hk), F32)] * 3,
        compiler_params=_params("parallel"),
    )(i1t, i2t, gt)


def _peer_ffn_kernel(x_ref, dn_ref, up_ref, gd_ref, o_ref):
    j = pl.program_id(1)
    tm = x_ref.shape[0]
    nk = gd_ref.shape[1]
    per = SUBLANES // 2
    half = o_ref.shape[1] // 2

    @pl.when(j == 0)
    def _():
        o_ref[...] = jnp.zeros(o_ref.shape, F32)

    def step(part):
        s = jnp.dot(x_ref[...], dn_ref[...], preferred_element_type=F32)
        pieces = []
        for c in range(per):
            sc = s[:, c * nk:(c + 1) * nk]
            act = 0.5 * sc * (1.0 + lax.erf(sc * (2.0 ** -0.5)))
            gate = gd_ref[pl.ds(part * per + c, tm, stride=SUBLANES), :]
            pieces.append((act * gate).astype(BF16))
        a = jnp.concatenate(pieces, axis=1)
        for n in range(2):
            cols = slice(n * half, (n + 1) * half)
            o_ref[:, cols] += jnp.dot(a, up_ref[:, cols], preferred_element_type=F32)

    for part in range(2):
        pl.when(lax.rem(j, 2) == part)(functools.partial(step, part))


def _peer_ffn(n2, down, up, gd3):
    t, d = n2.shape
    ne = up.shape[0]
    nk = gd3.shape[2]
    tm = _tile(t, 1024)
    te = SUBLANES * nk // 2
    nj = ne // te
    assert nj % 2 == 0
    down_t = down.reshape(nj, te, d).transpose(0, 2, 1)
    return pl.pallas_call(
        _peer_ffn_kernel,
        name="peer_ffn",
        grid=(t // tm, nj),
        in_specs=[pl.BlockSpec((tm, d), lambda i, j: (i, 0)),
                  pl.BlockSpec((None, d, te), lambda i, j: (j, 0, 0)),
                  pl.BlockSpec((te, d), lambda i, j: (j, 0)),
                  pl.BlockSpec((None, tm * SUBLANES, nk), lambda i, j: (j // 2, i, 0))],
        out_specs=pl.BlockSpec((tm, d), lambda i, j: (i, 0)),
        out_shape=jax.ShapeDtypeStruct((t, d), F32),
        compiler_params=_params("parallel", "arbitrary", vmem=VMEM_LIMIT_FFN),
    )(n2, down_t, up, gd3)


def _hybrid_mixer(h, n, l, bsz, seq, w_in, conv_a_w, ssd_conv_w, ssd_conv_b, ssd_dt_bias,
                  ssd_a_log, ssd_d, ssd_norm_w, w_branch_a, w_branch_b, w_branch_c, w_out):
    d = h.shape[1]
    conv_w = conv_a_w.shape[2]
    sb_w = w_branch_b.shape[1]
    ssd_w = w_branch_c.shape[1]
    xbc_w = ssd_conv_w.shape[2]
    heads = ssd_d.shape[1]
    bounds = [0, 3 * conv_w, 3 * sb_w, ssd_w, xbc_w, heads, 3 * d]
    offs = [sum(bounds[:k + 1]) for k in range(len(bounds))]
    w = w_in[l]
    w_a, w_b, w_z, w_xbc, w_dt, w_g = (w[:, offs[k]:offs[k + 1]].astype(BF16) for k in range(6))
    order = _ssd_group_major(ssd_w)
    w_xbc = w_xbc[:, order]
    w_dt = jnp.pad(w_dt, ((0, 0), (0, LANES - heads)))

    u_a = _matmul(n, w_a, BF16)
    u_b = _matmul(n, w_b, BF16)
    u_z = _matmul(n, w_z, BF16)
    u_xbc = _matmul(n, w_xbc, BF16)
    u_dt = _matmul(n, w_dt, F32)
    u_g = _matmul(n, w_g, BF16)

    mix_a = _short_conv(u_a, conv_a_w[l], bsz, seq)
    mix_b = _sb_attention(u_b, bsz, seq)
    xbc_act = _ssd_conv(u_xbc, ssd_conv_w[l][:, order], ssd_conv_b[l][order], bsz, seq)
    mix_c = _ssd(xbc_act, u_z, u_dt, ssd_dt_bias[l], ssd_a_log[l], ssd_d[l], ssd_norm_w[l], bsz, seq)

    m = _matmul(mix_a, w_branch_a[l].astype(BF16), F32, gate=u_g)
    m = _matmul(mix_b, w_branch_b[l].astype(BF16), F32, gate=u_g, gate_col=d, add=m)
    m = _matmul(mix_c, w_branch_c[l].astype(BF16), BF16, gate=u_g, gate_col=2 * d, add=m, tm=512)
    return _matmul(m, w_out[l].astype(BF16), F32, add=h)


def _peer(h, l, norm_ffn_w, peer_w_query, peer_sub_keys, peer_down, peer_up):
    n2 = _rmsnorm(h, norm_ffn_w[l], BF16)
    q = _matmul(n2, peer_w_query[l].astype(BF16), BF16)
    i1t, i2t, gt = _peer_topk(q, peer_sub_keys[l])
    gd3 = _peer_gates(i1t, i2t, gt, peer_sub_keys.shape[3])
    return _peer_ffn(n2, peer_down[l].astype(BF16), peer_up[l].astype(BF16), gd3)


def kernel(x, norm_mix_w, w_in, conv_a_w, ssd_conv_w, ssd_conv_b, ssd_dt_bias, ssd_a_log, ssd_d,
           ssd_norm_w, w_branch_a, w_branch_b, w_branch_c, w_out, norm_ffn_w, peer_w_query,
           peer_sub_keys, peer_down, peer_up, final_norm_w):
    bsz, seq, d = x.shape
    h = x.reshape(bsz * seq, d)
    ffn = None
    for l in range(w_in.shape[0]):
        if ffn is None:
            n = _rmsnorm(h, norm_mix_w[l], BF16)
        else:
            h, n = _rmsnorm(h, norm_mix_w[l], BF16, add=ffn)
        h = _hybrid_mixer(h, n, l, bsz, seq, w_in, conv_a_w, ssd_conv_w, ssd_conv_b,
                          ssd_dt_bias, ssd_a_log, ssd_d, ssd_norm_w, w_branch_a, w_branch_b,
                          w_branch_c, w_out)
        ffn = _peer(h, l, norm_ffn_w, peer_w_query, peer_sub_keys, peer_down, peer_up)
    return _rmsnorm(h, final_norm_w, x.dtype, add=ffn, keep_sum=False).reshape(bsz, seq, d)
```

```python
import functools

import jax
import jax.numpy as jnp
from jax import lax
from jax.experimental import pallas as pl
from jax.experimental.pallas import tpu as pltpu

F32 = jnp.float32
BF16 = jnp.bfloat16
I32 = jnp.int32

EPS = 1e-6
LOG2E = 1.4426950408889634
SB_HEAD_DIM = 128
SSD_HEAD_DIM = 64
SSD_GROUPS = 8
SSD_STATE = 128
SSD_CHUNK = 128
SSD_CHUNKS_PER_STEP = 8
GATES_UNROLL = 128
PEER_HEADS = 8
PEER_TOPK = 16

LANES = 128
SUBLANES = 8
VMEM_LIMIT = 48 * 1024 * 1024
VMEM_LIMIT_FFN = 56 * 1024 * 1024


def _params(*sem, vmem=VMEM_LIMIT):
    return pltpu.CompilerParams(dimension_semantics=sem, vmem_limit_bytes=vmem)


def _log2(n):
    assert n & (n - 1) == 0
    return n.bit_length() - 1


def _tile(n, pref):
    if n <= pref:
        return n
    t = pref
    while n % t:
        t -= SUBLANES
    return t


def _rmsnorm_kernel(x_ref, w_ref, o_ref):
    x = x_ref[...].astype(F32)
    ms = jnp.mean(x * x, axis=-1, keepdims=True)
    o_ref[...] = (x * lax.rsqrt(ms + EPS) * w_ref[...]).astype(o_ref.dtype)


def _add_rmsnorm_kernel(x_ref, a_ref, w_ref, *out_refs):
    h = x_ref[...] + a_ref[...]
    if len(out_refs) == 2:
        out_refs[0][...] = h
    ms = jnp.mean(h * h, axis=-1, keepdims=True)
    out_refs[-1][...] = (h * lax.rsqrt(ms + EPS) * w_ref[...]).astype(out_refs[-1].dtype)


def _rmsnorm(x, w, out_dtype, add=None, keep_sum=True):
    t, d = x.shape
    tm = _tile(t, 512)
    row = pl.BlockSpec((tm, d), lambda i: (i, 0))
    wspec = pl.BlockSpec((1, d), lambda i: (0, 0))
    normed = jax.ShapeDtypeStruct((t, d), out_dtype)
    w2 = w.reshape(1, d).astype(F32)
    if add is None:
        return pl.pallas_call(
            _rmsnorm_kernel,
            name="rmsnorm",
            grid=(t // tm,),
            in_specs=[row, wspec],
            out_specs=row,
            out_shape=normed,
            compiler_params=_params("parallel"),
        )(x, w2)
    return pl.pallas_call(
        _add_rmsnorm_kernel,
        name="add_rmsnorm",
        grid=(t // tm,),
        in_specs=[row, row, wspec],
        out_specs=[row, row] if keep_sum else row,
        out_shape=[jax.ShapeDtypeStruct((t, d), F32), normed] if keep_sum else normed,
        compiler_params=_params("parallel"),
    )(x, add, w2)


def _mm_kernel(*refs, has_gate, has_add):
    x_ref, w_ref = refs[0], refs[1]
    o_ref = refs[-1]
    acc = jnp.dot(x_ref[...], w_ref[...], preferred_element_type=F32)
    k = 2
    if has_gate:
        acc = acc * jax.nn.sigmoid(refs[k][...].astype(F32))
        k += 1
    if has_add:
        acc = acc + refs[k][...].astype(F32)
    o_ref[...] = acc.astype(o_ref.dtype)


def _matmul(x, w, out_dtype, gate=None, gate_col=0, add=None, tm=1024, tn=1024, name="matmul"):
    m, kd = x.shape
    n = w.shape[1]
    tm = _tile(m, tm)
    tn = min(tn, n)
    assert n % tn == 0 and gate_col % tn == 0
    goff = gate_col // tn
    ins = [x, w]
    specs = [pl.BlockSpec((tm, kd), lambda i, j: (i, 0)),
             pl.BlockSpec((kd, tn), lambda i, j: (0, j))]
    if gate is not None:
        ins.append(gate)
        specs.append(pl.BlockSpec((tm, tn), lambda i, j: (i, goff + j)))
    if add is not None:
        ins.append(add)
        specs.append(pl.BlockSpec((tm, tn), lambda i, j: (i, j)))
    return pl.pallas_call(
        functools.partial(_mm_kernel, has_gate=gate is not None, has_add=add is not None),
        name=name,
        grid=(m // tm, n // tn),
        in_specs=specs,
        out_specs=pl.BlockSpec((tm, tn), lambda i, j: (i, j)),
        out_shape=jax.ShapeDtypeStruct((m, n), out_dtype),
        compiler_params=_params("parallel", "parallel"),
    )(*ins)


CONV_SEQ_TILE = 256
CONV_COL_TILE = 2048


def _conv_taps(cur, buf_ref, w_ref, first):
    ts = cur.shape[0]
    kw = w_ref.shape[0]

    @pl.when(first)
    def _():
        buf_ref[0:SUBLANES, :] = jnp.zeros((SUBLANES, cur.shape[1]), F32)

    buf_ref[SUBLANES:SUBLANES + ts, :] = cur
    y = cur * w_ref[kw - 1:kw, :]
    for d in range(1, kw):
        y = y + buf_ref[SUBLANES - d:SUBLANES - d + ts, :] * w_ref[kw - 1 - d:kw - d, :]
    buf_ref[0:SUBLANES, :] = cur[ts - SUBLANES:ts, :]
    return y


def _short_conv_kernel(b_ref, c_ref, h_ref, w_ref, o_ref, buf_ref):
    cur = c_ref[...].astype(F32) * h_ref[...].astype(F32)
    y = _conv_taps(cur, buf_ref, w_ref, pl.program_id(2) == 0)
    o_ref[...] = (b_ref[...].astype(F32) * y).astype(o_ref.dtype)


def _short_conv(u_a, conv_w, bsz, seq):
    t = u_a.shape[0]
    width = conv_w.shape[1]
    ts = _tile(seq, CONV_SEQ_TILE)
    tc = min(CONV_COL_TILE, width)
    nc = width // tc
    ns = seq // ts
    row = lambda b, j, s: b * ns + s
    return pl.pallas_call(
        _short_conv_kernel,
        name="short_conv",
        grid=(bsz, nc, ns),
        in_specs=[pl.BlockSpec((ts, tc), lambda b, j, s: (row(b, j, s), j)),
                  pl.BlockSpec((ts, tc), lambda b, j, s: (row(b, j, s), nc + j)),
                  pl.BlockSpec((ts, tc), lambda b, j, s: (row(b, j, s), 2 * nc + j)),
                  pl.BlockSpec((conv_w.shape[0], tc), lambda b, j, s: (0, j))],
        out_specs=pl.BlockSpec((ts, tc), lambda b, j, s: (row(b, j, s), j)),
        out_shape=jax.ShapeDtypeStruct((t, width), BF16),
        scratch_shapes=[pltpu.VMEM((SUBLANES + ts, tc), F32)],
        compiler_params=_params("parallel", "parallel", "arbitrary"),
    )(u_a, u_a, u_a, conv_w.astype(F32))


CONV_ROWS = 128
CONV_HALO = 16


def _ssd_conv_kernel(x_ref, w_ref, b_ref, o_ref, buf_ref):
    ts, tc = x_ref.shape
    kw = w_ref.shape[0]

    @pl.when(pl.program_id(2) == 0)
    def _():
        buf_ref[0:CONV_HALO, :] = jnp.zeros((CONV_HALO, tc), BF16)

    buf_ref[CONV_HALO:CONV_HALO + ts, :] = x_ref[...]
    row = lax.broadcasted_iota(I32, (CONV_ROWS, CONV_ROWS + CONV_HALO), 0)
    col = lax.broadcasted_iota(I32, (CONV_ROWS, CONV_ROWS + CONV_HALO), 1)
    shift = [jnp.where(col == row + (CONV_HALO - d), 1.0, 0.0).astype(BF16) for d in range(1, kw)]
    outs = []
    for r in range(ts // CONV_ROWS):
        ext = buf_ref[r * CONV_ROWS:(r + 1) * CONV_ROWS + CONV_HALO, :]
        y = ext[CONV_HALO:, :].astype(F32) * w_ref[kw - 1:kw, :] + b_ref[...]
        for d in range(1, kw):
            y = y + jnp.dot(shift[d - 1], ext, preferred_element_type=F32) * w_ref[kw - 1 - d:kw - d, :]
        outs.append((y * jax.nn.sigmoid(y)).astype(o_ref.dtype))
    o_ref[...] = jnp.concatenate(outs, axis=0)
    buf_ref[0:CONV_HALO, :] = x_ref[ts - CONV_HALO:ts, :]


def _ssd_conv(xbc, conv_w, conv_b, bsz, seq):
    t, width = xbc.shape
    ts = _tile(seq, CONV_SEQ_TILE)
    tc = min(CONV_COL_TILE, width)
    nc = width // tc
    ns = seq // ts
    return pl.pallas_call(
        _ssd_conv_kernel,
        name="ssd_conv",
        grid=(bsz, nc, ns),
        in_specs=[pl.BlockSpec((ts, tc), lambda b, j, s: (b * ns + s, j)),
                  pl.BlockSpec((conv_w.shape[0], tc), lambda b, j, s: (0, j)),
                  pl.BlockSpec((1, tc), lambda b, j, s: (0, j))],
        out_specs=pl.BlockSpec((ts, tc), lambda b, j, s: (b * ns + s, j)),
        out_shape=jax.ShapeDtypeStruct((t, width), BF16),
        scratch_shapes=[pltpu.VMEM((CONV_HALO + ts, tc), BF16)],
        compiler_params=_params("parallel", "parallel", "arbitrary"),
    )(xbc, conv_w.astype(F32), conv_b.reshape(1, width).astype(F32))


def _split2(x):
    hi = x.astype(BF16)
    lo = (x - hi.astype(F32)).astype(BF16)
    return hi, lo


def _split3(x):
    hi = x.astype(BF16)
    r = x - hi.astype(F32)
    mid = r.astype(BF16)
    lo = (r - mid.astype(F32)).astype(BF16)
    return hi, mid, lo


def _sb_attn_kernel(q_ref, k_ref, v_ref, o_ref, qs_ref, oacc_ref, run_ref, *, bq, bk, scale):
    i = pl.program_id(2)
    dh = SB_HEAD_DIM
    nh = q_ref.shape[1] // dh
    nkb = bq // bk
    krow = lax.broadcasted_iota(I32, (bk, bk), 0)
    kcol = lax.broadcasted_iota(I32, (bk, bk), 1)
    later_mat = jnp.where(krow > kcol, 1.0, 0.0).astype(BF16)
    for h in range(nh):
        qs_ref[h] = (q_ref[:, h * dh:(h + 1) * dh].astype(F32) * (scale * LOG2E)).astype(BF16)

    def tile(h, start, run, mask, r0=0):
        ks = k_ref[pl.ds(start, bk), h * dh:(h + 1) * dh]
        vs = v_ref[pl.ds(start, bk), h * dh:(h + 1) * dh]
        z = lax.dot_general(qs_ref[h, r0:, :], ks, (((1,), (1,)), ((), ())), preferred_element_type=F32)
        z = z.astype(BF16)
        neg = jnp.minimum(z, 0.0)
        pos = z - neg
        ls = neg - jnp.log(1.0 + jnp.exp2(neg - pos)) * LOG2E
        lnb = ls - z
        if mask is not None:
            lnb = jnp.where(mask, lnb.astype(F32), 0.0).astype(BF16)
        later = jnp.dot(lnb, later_mat, preferred_element_type=F32)
        a = jnp.exp2(ls.astype(F32) + later + jnp.concatenate([run] * (bk // LANES), axis=1))
        if mask is not None:
            a = jnp.where(mask, a, 0.0)
        o = jnp.dot(a.astype(BF16), vs, preferred_element_type=F32)
        total = later[:, 0:1] + lnb[:, 0:1].astype(F32)
        return o, run + jnp.broadcast_to(total, run.shape)

    oacc_ref[...] = jnp.zeros(oacc_ref.shape, F32)
    run_ref[...] = jnp.zeros(run_ref.shape, F32)
    for d in range(nkb):
        r0 = (nkb - 1 - d) * bk
        start = pl.multiple_of(i * bq + r0, bk)
        mask = (lax.broadcasted_iota(I32, (bq - r0, bk), 1) < lax.broadcasted_iota(I32, (bq - r0, bk), 0))
        for h in range(nh):
            o, run = tile(h, start, run_ref[h, r0:, :], mask, r0)
            oacc_ref[h, r0:, :] += o
            run_ref[h, r0:, :] = run
    state = [(oacc_ref[h], run_ref[h]) for h in range(nh)]

    def body(jj, carry):
        out = list(carry)
        for d in range(nkb):
            start = pl.multiple_of(((i - jj) * nkb - 1 - d) * bk, bk)
            for h in range(nh):
                o, run = tile(h, start, out[h][1], None)
                out[h] = (out[h][0] + o, run)
        return tuple(out)

    res = lax.fori_loop(0, i, body, tuple(state))
    o_ref[...] = jnp.concatenate([r[0] for r in res], axis=1).astype(o_ref.dtype)


SB_HEADS_PER_STEP = 2
SB_Q_BLOCK = 1024
SB_K_BLOCK = 256


def _sb_attention(u_b, bsz, seq):
    t, w3 = u_b.shape
    width = w3 // 3
    hw = SB_HEADS_PER_STEP * SB_HEAD_DIM
    groups = width // hw
    bq = _tile(seq, SB_Q_BLOCK)
    bk = min(SB_K_BLOCK, bq)
    assert bq % bk == 0
    nq = seq // bq
    return pl.pallas_call(
        functools.partial(_sb_attn_kernel, bq=bq, bk=bk, scale=SB_HEAD_DIM ** -0.5),
        grid=(bsz, groups, nq),
        in_specs=[pl.BlockSpec((bq, hw), lambda b, h, i: (b * nq + i, h)),
                  pl.BlockSpec((seq, hw), lambda b, h, i: (b, groups + h)),
                  pl.BlockSpec((seq, hw), lambda b, h, i: (b, 2 * groups + h))],
        out_specs=pl.BlockSpec((bq, hw), lambda b, h, i: (b * nq + i, h)),
        out_shape=jax.ShapeDtypeStruct((t, width), BF16),
        scratch_shapes=[pltpu.VMEM((SB_HEADS_PER_STEP, bq, SB_HEAD_DIM), BF16),
                        pltpu.VMEM((SB_HEADS_PER_STEP, bq, SB_HEAD_DIM), F32),
                        pltpu.VMEM((SB_HEADS_PER_STEP, bq, LANES), F32)],
        name="sb_attention",
        compiler_params=_params("parallel", "parallel", "arbitrary"),
    )(u_b, u_b, u_b)


def _ssd_kernel(xbc_ref, z_ref, dt_ref, dtb_ref, alog_ref, dsk_ref, nw_ref,
                o_ref, state_ref, acst_ref, *, hpg, hd, lc):
    g = pl.program_id(1)
    gw = hpg * hd
    nsub = xbc_ref.shape[0] // lc
    ns = (xbc_ref.shape[1] - gw) // 2

    @pl.when(pl.program_id(2) == 0)
    def _():
        state_ref[...] = jnp.zeros(state_ref.shape, F32)

    row = lax.broadcasted_iota(I32, (lc, lc), 0)
    col = lax.broadcasted_iota(I32, (lc, lc), 1)
    causal = row >= col
    tri = jnp.where(causal, 1.0, 0.0).astype(BF16)
    hrow = lax.broadcasted_iota(I32, (LANES, gw), 0)
    hcol = lax.broadcasted_iota(I32, (LANES, gw), 1)
    spread = jnp.where(hrow == g * hpg + (hcol >> _log2(hd)), 1.0, 0.0).astype(BF16)
    hrow2 = lax.broadcasted_iota(I32, (LANES, hpg * lc), 0)
    hcol2 = lax.broadcasted_iota(I32, (LANES, hpg * lc), 1)
    spread_l = jnp.where(hrow2 == g * hpg + (hcol2 >> _log2(lc)), 1.0, 0.0).astype(BF16)

    def expand(x, parts, mat):
        pieces = (_split3(x) if parts == 3 else _split2(x) if parts == 2 else (x.astype(BF16),))
        return sum(jnp.dot(p, mat, preferred_element_type=F32) for p in pieces)

    neg_a = -jnp.exp(alog_ref[...])
    dsk_x = expand(jnp.broadcast_to(dsk_ref[...], (SUBLANES, LANES)), 3, spread)[0:1, :]

    def chunk(sc, prev):
        rows = slice(sc * lc, (sc + 1) * lc)
        dt = jax.nn.softplus(dt_ref[rows, :] + dtb_ref[...])
        a_cs = sum(jnp.dot(tri, part, preferred_element_type=F32) for part in _split3(dt * neg_a))
        acst_ref[sc] = a_cs.T
        exp_acs = jnp.exp(a_cs)
        dte = jnp.exp(a_cs[lc - 1:lc, :] - a_cs)

        dt_x = expand(dt, 1, spread)
        dte_x = expand(dte, 1, spread)
        eacs_x = expand(exp_acs, 2, spread)
        acs_col = expand(a_cs, 2, spread_l)

        xs = xbc_ref[rows, 0:gw].astype(F32)
        bm = xbc_ref[rows, gw:gw + ns]
        cm = xbc_ref[rows, gw + ns:gw + 2 * ns]
        xdt = xs * dt_x
        xdt_b = xdt.astype(BF16)
        cb = lax.dot_general(cm, bm, (((1,), (1,)), ((), ())), preferred_element_type=F32)

        y_parts = []
        for r in range(hpg):
            a_l = acs_col[:, r * lc:(r + 1) * lc]
            a_s = acst_ref[sc, pl.ds(g * hpg + r, 1), :]
            seg = jnp.where(causal, jnp.exp(jnp.where(causal, a_l - a_s, 0.0)), 0.0)
            m = (cb * seg).astype(BF16)
            y_parts.append(jnp.dot(m, xdt_b[:, r * hd:(r + 1) * hd], preferred_element_type=F32))
        y_diag = jnp.concatenate(y_parts, axis=1)

        y_off = jnp.dot(cm, prev.astype(BF16), preferred_element_type=F32) * eacs_x
        y = y_diag + y_off + xs * dsk_x

        bm_t = bm.astype(F32).T.astype(BF16)
        new_state = jnp.dot(bm_t, (xdt * dte_x).astype(BF16), preferred_element_type=F32)

        zf = z_ref[rows, :].astype(F32)
        yg = y * (zf * jax.nn.sigmoid(zf))
        ms = jnp.mean(yg * yg, axis=-1, keepdims=True)
        o_ref[rows, :] = (yg * lax.rsqrt(ms + EPS) * nw_ref[...]).astype(o_ref.dtype)
        return prev * eacs_x[lc - 1:lc, :] + new_state

    state = state_ref[...]
    for sc in range(nsub):
        state = chunk(sc, state)
    state_ref[...] = state


def _pad_lanes(v):
    return jnp.zeros((1, LANES), F32).at[0, :v.shape[0]].set(v.astype(F32))


def _ssd_group_major(width):
    gw = width // SSD_GROUPS
    base_b = width
    base_c = width + SSD_GROUPS * SSD_STATE
    cols = []
    for g in range(SSD_GROUPS):
        cols += list(range(g * gw, (g + 1) * gw))
        cols += list(range(base_b + g * SSD_STATE, base_b + (g + 1) * SSD_STATE))
        cols += list(range(base_c + g * SSD_STATE, base_c + (g + 1) * SSD_STATE))
    return jnp.asarray(cols, I32)


def _ssd(xbc_act, z, dt_raw, dt_bias, a_log, d_skip, norm_w, bsz, seq):
    t, width = z.shape
    heads = width // SSD_HEAD_DIM
    hpg = heads // SSD_GROUPS
    gw = hpg * SSD_HEAD_DIM
    nsub = SSD_CHUNKS_PER_STEP if seq % (SSD_CHUNKS_PER_STEP * SSD_CHUNK) == 0 else 1
    lc = nsub * SSD_CHUNK
    nch = seq // lc
    assert gw * SSD_GROUPS == width and SSD_STATE == LANES and heads <= LANES
    row = lambda b, g, c: b * nch + c
    return pl.pallas_call(
        functools.partial(_ssd_kernel, hpg=hpg, hd=SSD_HEAD_DIM, lc=SSD_CHUNK),
        name="ssd",
        grid=(bsz, SSD_GROUPS, nch),
        in_specs=[pl.BlockSpec((lc, gw + 2 * SSD_STATE), lambda b, g, c: (row(b, g, c), g)),
                  pl.BlockSpec((lc, gw), lambda b, g, c: (row(b, g, c), g)),
                  pl.BlockSpec((lc, LANES), lambda b, g, c: (row(b, g, c), 0)),
                  pl.BlockSpec((1, LANES), lambda b, g, c: (0, 0)),
                  pl.BlockSpec((1, LANES), lambda b, g, c: (0, 0)),
                  pl.BlockSpec((1, LANES), lambda b, g, c: (0, 0)),
                  pl.BlockSpec((1, gw), lambda b, g, c: (0, g))],
        out_specs=pl.BlockSpec((lc, gw), lambda b, g, c: (row(b, g, c), g)),
        out_shape=jax.ShapeDtypeStruct((t, width), BF16),
        scratch_shapes=[pltpu.VMEM((SSD_STATE, gw), F32), pltpu.VMEM((nsub, LANES, SSD_CHUNK), F32)],
        compiler_params=_params("parallel", "parallel", "arbitrary"),
    )(xbc_act, z, dt_raw, _pad_lanes(dt_bias), _pad_lanes(a_log),
      _pad_lanes(d_skip), norm_w.reshape(1, width).astype(F32))


def _topk_rows(s, k):
    n = s.shape[0]
    iota = lax.broadcasted_iota(I32, s.shape, 0)
    vals, idxs = [], []
    for it in range(k):
        m = jnp.max(s, axis=0, keepdims=True)
        i = jnp.min(jnp.where(s == m, iota, n), axis=0, keepdims=True)
        vals.append(m)
        idxs.append(i)
        if it + 1 < k:
            s = jnp.where(iota == i, -jnp.inf, s)
    return jnp.concatenate(vals, axis=0), jnp.concatenate(idxs, axis=0)


def _peer_topk_kernel(q_ref, keys_ref, i1_ref, i2_ref, g_ref, *, topk):
    half = q_ref.shape[1] // 2
    sub_v, sub_i = [], []
    for c in range(2):
        s = lax.dot_general(keys_ref[c], q_ref[:, c * half:(c + 1) * half],
                            (((1,), (1,)), ((), ())), preferred_element_type=F32)
        v, i = _topk_rows(s, topk)
        sub_v.append(v)
        sub_i.append(i)
    n_wide = topk // 2
    pieces, starts = [], []
    for a in range(n_wide):
        nb = min(topk, -(-(topk // (a + 1)) // SUBLANES) * SUBLANES)
        starts.append(sum(p.shape[0] for p in pieces))
        pieces.append(sub_v[0][a:a + 1, :] + sub_v[1][0:nb, :])
    tail_start = sum(p.shape[0] for p in pieces)
    pieces.append(sub_v[0][n_wide:topk, :] + sub_v[1][0:1, :])
    top_s, pos = _topk_rows(jnp.concatenate(pieces, axis=0), topk)
    pa = jnp.zeros(pos.shape, I32)
    pb = pos
    for a in range(1, n_wide):
        ge = pos >= starts[a]
        pa = jnp.where(ge, a, pa)
        pb = jnp.where(ge, pos - starts[a], pb)
    ge = pos >= tail_start
    pa = jnp.where(ge, pos + (n_wide - tail_start), pa)
    pb = jnp.where(ge, 0, pb)
    i1 = jnp.zeros(pos.shape, I32)
    i2 = jnp.zeros(pos.shape, I32)
    for a in range(topk):
        i1 = jnp.where(pa == a, sub_i[0][a:a + 1, :], i1)
        i2 = jnp.where(pb == a, sub_i[1][a:a + 1, :], i2)
    e = jnp.exp(top_s - top_s[0:1, :])
    i1_ref[...] = i1
    i2_ref[...] = i2
    g_ref[...] = e / jnp.sum(e, axis=0, keepdims=True)


def _peer_topk(q, sub_keys):
    t, qw = q.shape
    heads, _, nkeys, dk2 = sub_keys.shape
    tt = _tile(t, 1024)
    hk = heads * PEER_TOPK
    out = jax.ShapeDtypeStruct((hk, t), I32)
    spec = pl.BlockSpec((PEER_TOPK, tt), lambda i, h: (h, i))
    return pl.pallas_call(
        functools.partial(_peer_topk_kernel, topk=PEER_TOPK),
        name="peer_topk",
        grid=(t // tt, heads),
        in_specs=[pl.BlockSpec((tt, 2 * dk2), lambda i, h: (i, h)),
                  pl.BlockSpec((None, 2, nkeys, dk2), lambda i, h: (h, 0, 0, 0))],
        out_specs=[spec, spec, spec],
        out_shape=[out, out, jax.ShapeDtypeStruct((hk, t), F32)],
        compiler_params=_params("parallel", "parallel"),
    )(q, sub_keys.astype(BF16))


def _peer_gates_kernel(i1_ref, i2_ref, g_ref, o_ref, i1s_ref, i2s_ref, gs_ref):
    nk = o_ref.shape[2]
    i1s_ref[...] = i1_ref[...].astype(F32).T
    i2s_ref[...] = i2_ref[...].astype(F32).T
    gs_ref[...] = g_ref[...].T
    key = lax.broadcasted_iota(I32, (nk, i1_ref.shape[0]), 0).astype(F32)

    def body(t, _):
        r1 = i1s_ref[pl.ds(t, 1), :]
        r2 = i2s_ref[pl.ds(t, 1), :]
        gg = gs_ref[pl.ds(t, 1), :]
        a_mat = jnp.where(key == r1, gg, 0.0).astype(BF16)
        b_mat = jnp.where(key == r2, 1.0, 0.0).astype(BF16)
        gmap = lax.dot_general(a_mat, b_mat, (((1,), (1,)), ((), ())), preferred_element_type=F32)
        o_ref[:, pl.ds(pl.multiple_of(t * SUBLANES, SUBLANES), SUBLANES), :] = gmap.reshape(
            nk // SUBLANES, SUBLANES, nk)
        return 0

    lax.fori_loop(0, i1s_ref.shape[0], body, 0, unroll=GATES_UNROLL)


def _peer_gates(i1t, i2t, gt, nkeys):
    hk, t = i1t.shape
    tt = _tile(t, 128)
    spec = pl.BlockSpec((hk, tt), lambda i: (0, i))
    return pl.pallas_call(
        _peer_gates_kernel,
        name="peer_gates",
        grid=(t // tt,),
        in_specs=[spec, spec, spec],
        out_specs=pl.BlockSpec((nkeys // SUBLANES, tt * SUBLANES, nkeys), lambda i: (0, i, 0)),
        out_shape=jax.ShapeDtypeStruct((nkeys // SUBLANES, t * SUBLANES, nkeys), F32),
        scratch_shapes=[pltpu.VMEM((tt, hk), F32)] * 3,
        compiler_params=_params("parallel"),
    )(i1t, i2t, gt)


def _peer_ffn_kernel(x_ref, dn_ref, up_ref, gd_ref, o_ref):
    j = pl.program_id(1)
    tm = x_ref.shape[0]
    nk = gd_ref.shape[1]
    per = SUBLANES // 2
    half = o_ref.shape[1] // 2

    @pl.when(j == 0)
    def _():
        o_ref[...] = jnp.zeros(o_ref.shape, F32)

    def step(part):
        s = jnp.dot(x_ref[...], dn_ref[...], preferred_element_type=F32)
        pieces = []
        for c in range(per):
            sc = s[:, c * nk:(c + 1) * nk].astype(BF16)
            act = 0.5 * sc * (1.0 + lax.erf(sc * (2.0 ** -0.5)))
            gate = gd_ref[pl.ds(part * per + c, tm, stride=SUBLANES), :]
            pieces.append(act * gate.astype(BF16))
        a = jnp.concatenate(pieces, axis=1)
        for n in range(2):
            cols = slice(n * half, (n + 1) * half)
            o_ref[:, cols] += jnp.dot(a, up_ref[:, cols], preferred_element_type=F32)

    for part in range(2):
        pl.when(lax.rem(j, 2) == part)(functools.partial(step, part))


def _peer_ffn(n2, down, up, gd3):
    t, d = n2.shape
    ne = up.shape[0]
    nk = gd3.shape[2]
    tm = _tile(t, 1024)
    te = SUBLANES * nk // 2
    nj = ne // te
    assert nj % 2 == 0
    down_t = down.reshape(nj, te, d).transpose(0, 2, 1)
    return pl.pallas_call(
        _peer_ffn_kernel,
        name="peer_ffn",
        grid=(t // tm, nj),
        in_specs=[pl.BlockSpec((tm, d), lambda i, j: (i, 0)),
                  pl.BlockSpec((None, d, te), lambda i, j: (j, 0, 0)),
                  pl.BlockSpec((te, d), lambda i, j: (j, 0)),
                  pl.BlockSpec((None, tm * SUBLANES, nk), lambda i, j: (j // 2, i, 0))],
        out_specs=pl.BlockSpec((tm, d), lambda i, j: (i, 0)),
        out_shape=jax.ShapeDtypeStruct((t, d), F32),
        compiler_params=_params("parallel", "arbitrary", vmem=VMEM_LIMIT_FFN),
    )(n2, down_t, up, gd3)


def _hybrid_mixer(h, n, l, bsz, seq, w_in, conv_a_w, ssd_conv_w, ssd_conv_b, ssd_dt_bias,
                  ssd_a_log, ssd_d, ssd_norm_w, w_branch_a, w_branch_b, w_branch_c, w_out):
    d = h.shape[1]
    conv_w = conv_a_w.shape[2]
    sb_w = w_branch_b.shape[1]
    ssd_w = w_branch_c.shape[1]
    xbc_w = ssd_conv_w.shape[2]
    heads = ssd_d.shape[1]
    bounds = [0, 3 * conv_w, 3 * sb_w, ssd_w, xbc_w, heads, 3 * d]
    offs = [sum(bounds[:k + 1]) for k in range(len(bounds))]
    w = w_in[l]
    w_a, w_b, w_z, w_xbc, w_dt, w_g = (w[:, offs[k]:offs[k + 1]].astype(BF16) for k in range(6))
    order = _ssd_group_major(ssd_w)
    w_xbc = w_xbc[:, order]
    w_dt = jnp.pad(w_dt, ((0, 0), (0, LANES - heads)))

    u_a = _matmul(n, w_a, BF16)
    u_b = _matmul(n, w_b, BF16)
    u_z = _matmul(n, w_z, BF16)
    u_xbc = _matmul(n, w_xbc, BF16)
    u_dt = _matmul(n, w_dt, F32)
    u_g = _matmul(n, w_g, BF16)

    mix_a = _short_conv(u_a, conv_a_w[l], bsz, seq)
    mix_b = _sb_attention(u_b, bsz, seq)
    xbc_act = _ssd_conv(u_xbc, ssd_conv_w[l][:, order], ssd_conv_b[l][order], bsz, seq)
    mix_c = _ssd(xbc_act, u_z, u_dt, ssd_dt_bias[l], ssd_a_log[l], ssd_d[l], ssd_norm_w[l], bsz, seq)

    m = _matmul(mix_a, w_branch_a[l].astype(BF16), F32, gate=u_g)
    m = _matmul(mix_b, w_branch_b[l].astype(BF16), F32, gate=u_g, gate_col=d, add=m)
    m = _matmul(mix_c, w_branch_c[l].astype(BF16), BF16, gate=u_g, gate_col=2 * d, add=m, tm=512)
    return _matmul(m, w_out[l].astype(BF16), F32, add=h)


def _peer(h, l, norm_ffn_w, peer_w_query, peer_sub_keys, peer_down, peer_up):
    n2 = _rmsnorm(h, norm_ffn_w[l], BF16)
    q = _matmul(n2, peer_w_query[l].astype(BF16), BF16)
    i1t, i2t, gt = _peer_topk(q, peer_sub_keys[l])
    gd3 = _peer_gates(i1t, i2t, gt, peer_sub_keys.shape[3])
    return _peer_ffn(n2, peer_down[l].astype(BF16), peer_up[l].astype(BF16), gd3)


def kernel(x, norm_mix_w, w_in, conv_a_w, ssd_conv_w, ssd_conv_b, ssd_dt_bias, ssd_a_log, ssd_d,
           ssd_norm_w, w_branch_a, w_branch_b, w_branch_c, w_out, norm_ffn_w, peer_w_query,
           peer_sub_keys, peer_down, peer_up, final_norm_w):
    bsz, seq, d = x.shape
    h = x.reshape(bsz * seq, d)
    ffn = None
    for l in range(w_in.shape[0]):
        if ffn is None:
            n = _rmsnorm(h, norm_mix_w[l], BF16)
        else:
            h, n = _rmsnorm(h, norm_mix_w[l], BF16, add=ffn)
        h = _hybrid_mixer(h, n, l, bsz, seq, w_in, conv_a_w, ssd_conv_w, ssd_conv_b,
                          ssd_dt_bias, ssd_a_log, ssd_d, ssd_norm_w, w_branch_a, w_branch_b,
                          w_branch_c, w_out)
        ffn = _peer(h, l, norm_ffn_w, peer_w_query, peer_sub_keys, peer_down, peer_up)
    return _rmsnorm(h, final_norm_w, x.dtype, add=ffn, keep_sum=False).reshape(bsz, seq, d)
```

```python
import functools

import jax
import jax.numpy as jnp
from jax import lax
from jax.experimental import pallas as pl
from jax.experimental.pallas import tpu as pltpu

F32 = jnp.float32
BF16 = jnp.bfloat16
I32 = jnp.int32

EPS = 1e-6
LOG2E = 1.4426950408889634
SB_HEAD_DIM = 128
SSD_HEAD_DIM = 64
SSD_GROUPS = 8
SSD_STATE = 128
SSD_CHUNK = 128
SSD_CHUNKS_PER_STEP = 8
GATES_UNROLL = 128
PEER_HEADS = 8
PEER_TOPK = 16

LANES = 128
SUBLANES = 8
VMEM_LIMIT = 48 * 1024 * 1024
VMEM_LIMIT_FFN = 56 * 1024 * 1024


def _params(*sem, vmem=VMEM_LIMIT):
    return pltpu.CompilerParams(dimension_semantics=sem, vmem_limit_bytes=vmem)


def _log2(n):
    assert n & (n - 1) == 0
    return n.bit_length() - 1


def _tile(n, pref):
    if n <= pref:
        return n
    t = pref
    while n % t:
        t -= SUBLANES
    return t


def _rmsnorm_kernel(x_ref, w_ref, o_ref):
    x = x_ref[...].astype(F32)
    ms = jnp.mean(x * x, axis=-1, keepdims=True)
    o_ref[...] = (x * lax.rsqrt(ms + EPS) * w_ref[...]).astype(o_ref.dtype)


def _add_rmsnorm_kernel(x_ref, a_ref, w_ref, *out_refs):
    h = x_ref[...] + a_ref[...]
    if len(out_refs) == 2:
        out_refs[0][...] = h
    ms = jnp.mean(h * h, axis=-1, keepdims=True)
    out_refs[-1][...] = (h * lax.rsqrt(ms + EPS) * w_ref[...]).astype(out_refs[-1].dtype)


def _rmsnorm(x, w, out_dtype, add=None, keep_sum=True):
    t, d = x.shape
    tm = _tile(t, 512)
    row = pl.BlockSpec((tm, d), lambda i: (i, 0))
    wspec = pl.BlockSpec((1, d), lambda i: (0, 0))
    normed = jax.ShapeDtypeStruct((t, d), out_dtype)
    w2 = w.reshape(1, d).astype(F32)
    if add is None:
        return pl.pallas_call(
            _rmsnorm_kernel,
            name="rmsnorm",
            grid=(t // tm,),
            in_specs=[row, wspec],
            out_specs=row,
            out_shape=normed,
            compiler_params=_params("parallel"),
        )(x, w2)
    return pl.pallas_call(
        _add_rmsnorm_kernel,
        name="add_rmsnorm",
        grid=(t // tm,),
        in_specs=[row, row, wspec],
        out_specs=[row, row] if keep_sum else row,
        out_shape=[jax.ShapeDtypeStruct((t, d), F32), normed] if keep_sum else normed,
        compiler_params=_params("parallel"),
    )(x, add, w2)


def _mm_kernel(*refs, has_gate, has_add):
    x_ref, w_ref = refs[0], refs[1]
    o_ref = refs[-1]
    acc = jnp.dot(x_ref[...], w_ref[...], preferred_element_type=F32)
    k = 2
    if has_gate:
        acc = acc * jax.nn.sigmoid(refs[k][...].astype(F32))
        k += 1
    if has_add:
        acc = acc + refs[k][...].astype(F32)
    o_ref[...] = acc.astype(o_ref.dtype)


def _matmul(x, w, out_dtype, gate=None, gate_col=0, add=None, tm=1024, tn=1024, name="matmul"):
    m, kd = x.shape
    n = w.shape[1]
    tm = _tile(m, tm)
    tn = min(tn, n)
    assert n % tn == 0 and gate_col % tn == 0
    goff = gate_col // tn
    ins = [x, w]
    specs = [pl.BlockSpec((tm, kd), lambda i, j: (i, 0)),
             pl.BlockSpec((kd, tn), lambda i, j: (0, j))]
    if gate is not None:
        ins.append(gate)
        specs.append(pl.BlockSpec((tm, tn), lambda i, j: (i, goff + j)))
    if add is not None:
        ins.append(add)
        specs.append(pl.BlockSpec((tm, tn), lambda i, j: (i, j)))
    return pl.pallas_call(
        functools.partial(_mm_kernel, has_gate=gate is not None, has_add=add is not None),
        name=name,
        grid=(m // tm, n // tn),
        in_specs=specs,
        out_specs=pl.BlockSpec((tm, tn), lambda i, j: (i, j)),
        out_shape=jax.ShapeDtypeStruct((m, n), out_dtype),
        compiler_params=_params("parallel", "parallel"),
    )(*ins)


CONV_SEQ_TILE = 256
CONV_COL_TILE = 2048


def _conv_taps(cur, buf_ref, w_ref, first):
    ts = cur.shape[0]
    kw = w_ref.shape[0]

    @pl.when(first)
    def _():
        buf_ref[0:SUBLANES, :] = jnp.zeros((SUBLANES, cur.shape[1]), F32)

    buf_ref[SUBLANES:SUBLANES + ts, :] = cur
    y = cur * w_ref[kw - 1:kw, :]
    for d in range(1, kw):
        y = y + buf_ref[SUBLANES - d:SUBLANES - d + ts, :] * w_ref[kw - 1 - d:kw - d, :]
    buf_ref[0:SUBLANES, :] = cur[ts - SUBLANES:ts, :]
    return y


def _short_conv_kernel(b_ref, c_ref, h_ref, w_ref, o_ref, buf_ref):
    cur = c_ref[...].astype(F32) * h_ref[...].astype(F32)
    y = _conv_taps(cur, buf_ref, w_ref, pl.program_id(2) == 0)
    o_ref[...] = (b_ref[...].astype(F32) * y).astype(o_ref.dtype)


def _short_conv(u_a, conv_w, bsz, seq):
    t = u_a.shape[0]
    width = conv_w.shape[1]
    ts = _tile(seq, CONV_SEQ_TILE)
    tc = min(CONV_COL_TILE, width)
    nc = width // tc
    ns = seq // ts
    row = lambda b, j, s: b * ns + s
    return pl.pallas_call(
        _short_conv_kernel,
        name="short_conv",
        grid=(bsz, nc, ns),
        in_specs=[pl.BlockSpec((ts, tc), lambda b, j, s: (row(b, j, s), j)),
                  pl.BlockSpec((ts, tc), lambda b, j, s: (row(b, j, s), nc + j)),
                  pl.BlockSpec((ts, tc), lambda b, j, s: (row(b, j, s), 2 * nc + j)),
                  pl.BlockSpec((conv_w.shape[0], tc), lambda b, j, s: (0, j))],
        out_specs=pl.BlockSpec((ts, tc), lambda b, j, s: (row(b, j, s), j)),
        out_shape=jax.ShapeDtypeStruct((t, width), BF16),
        scratch_shapes=[pltpu.VMEM((SUBLANES + ts, tc), F32)],
        compiler_params=_params("parallel", "parallel", "arbitrary"),
    )(u_a, u_a, u_a, conv_w.astype(F32))


CONV_ROWS = 128
CONV_HALO = 16


def _ssd_conv_kernel(x_ref, w_ref, b_ref, o_ref, buf_ref):
    ts, tc = x_ref.shape
    kw = w_ref.shape[0]

    @pl.when(pl.program_id(2) == 0)
    def _():
        buf_ref[0:CONV_HALO, :] = jnp.zeros((CONV_HALO, tc), BF16)

    buf_ref[CONV_HALO:CONV_HALO + ts, :] = x_ref[...]
    row = lax.broadcasted_iota(I32, (CONV_ROWS, CONV_ROWS + CONV_HALO), 0)
    col = lax.broadcasted_iota(I32, (CONV_ROWS, CONV_ROWS + CONV_HALO), 1)
    shift = [jnp.where(col == row + (CONV_HALO - d), 1.0, 0.0).astype(BF16) for d in range(1, kw)]
    outs = []
    for r in range(ts // CONV_ROWS):
        ext = buf_ref[r * CONV_ROWS:(r + 1) * CONV_ROWS + CONV_HALO, :]
        y = ext[CONV_HALO:, :].astype(F32) * w_ref[kw - 1:kw, :] + b_ref[...]
        for d in range(1, kw):
            y = y + jnp.dot(shift[d - 1], ext, preferred_element_type=F32) * w_ref[kw - 1 - d:kw - d, :]
        outs.append((y * jax.nn.sigmoid(y)).astype(o_ref.dtype))
    o_ref[...] = jnp.concatenate(outs, axis=0)
    buf_ref[0:CONV_HALO, :] = x_ref[ts - CONV_HALO:ts, :]


def _ssd_conv(xbc, conv_w, conv_b, bsz, seq):
    t, width = xbc.shape
    ts = _tile(seq, CONV_SEQ_TILE)
    tc = min(CONV_COL_TILE, width)
    nc = width // tc
    ns = seq // ts
    return pl.pallas_call(
        _ssd_conv_kernel,
        name="ssd_conv",
        grid=(bsz, nc, ns),
        in_specs=[pl.BlockSpec((ts, tc), lambda b, j, s: (b * ns + s, j)),
                  pl.BlockSpec((conv_w.shape[0], tc), lambda b, j, s: (0, j)),
                  pl.BlockSpec((1, tc), lambda b, j, s: (0, j))],
        out_specs=pl.BlockSpec((ts, tc), lambda b, j, s: (b * ns + s, j)),
        out_shape=jax.ShapeDtypeStruct((t, width), BF16),
        scratch_shapes=[pltpu.VMEM((CONV_HALO + ts, tc), BF16)],
        compiler_params=_params("parallel", "parallel", "arbitrary"),
    )(xbc, conv_w.astype(F32), conv_b.reshape(1, width).astype(F32))


def _split2(x):
    hi = x.astype(BF16)
    lo = (x - hi.astype(F32)).astype(BF16)
    return hi, lo


def _split3(x):
    hi = x.astype(BF16)
    r = x - hi.astype(F32)
    mid = r.astype(BF16)
    lo = (r - mid.astype(F32)).astype(BF16)
    return hi, mid, lo


def _sb_attn_kernel(q_ref, k_ref, v_ref, o_ref, qs_ref, oacc_ref, run_ref, *, bq, bk, scale):
    i = pl.program_id(2)
    dh = SB_HEAD_DIM
    nh = q_ref.shape[1] // dh
    nkb = bq // bk
    krow = lax.broadcasted_iota(I32, (bk, bk), 0)
    kcol = lax.broadcasted_iota(I32, (bk, bk), 1)
    later_mat = jnp.where(krow > kcol, 1.0, 0.0).astype(BF16)
    for h in range(nh):
        qs_ref[h] = (q_ref[:, h * dh:(h + 1) * dh].astype(F32) * (scale * LOG2E)).astype(BF16)

    def tile(h, start, run, mask, r0=0):
        ks = k_ref[pl.ds(start, bk), h * dh:(h + 1) * dh]
        vs = v_ref[pl.ds(start, bk), h * dh:(h + 1) * dh]
        z = lax.dot_general(qs_ref[h, r0:, :], ks, (((1,), (1,)), ((), ())), preferred_element_type=F32)
        z = z.astype(BF16)
        neg = jnp.minimum(z, 0.0)
        pos = z - neg
        ls = neg - jnp.log(1.0 + jnp.exp2(neg - pos)) * LOG2E
        lnb = ls - z
        if mask is not None:
            lnb = jnp.where(mask, lnb.astype(F32), 0.0).astype(BF16)
        later = jnp.dot(lnb, later_mat, preferred_element_type=F32)
        a = jnp.exp2(ls.astype(F32) + later + jnp.concatenate([run] * (bk // LANES), axis=1))
        if mask is not None:
            a = jnp.where(mask, a, 0.0)
        o = jnp.dot(a.astype(BF16), vs, preferred_element_type=F32)
        total = later[:, 0:1] + lnb[:, 0:1].astype(F32)
        return o, run + jnp.broadcast_to(total, run.shape)

    oacc_ref[...] = jnp.zeros(oacc_ref.shape, F32)
    run_ref[...] = jnp.zeros(run_ref.shape, F32)
    for d in range(nkb):
        r0 = (nkb - 1 - d) * bk
        start = pl.multiple_of(i * bq + r0, bk)
        mask = (lax.broadcasted_iota(I32, (bq - r0, bk), 1) < lax.broadcasted_iota(I32, (bq - r0, bk), 0))
        for h in range(nh):
            o, run = tile(h, start, run_ref[h, r0:, :], mask, r0)
            oacc_ref[h, r0:, :] += o
            run_ref[h, r0:, :] = run
    state = [(oacc_ref[h], run_ref[h]) for h in range(nh)]

    def body(jj, carry):
        out = list(carry)
        for d in range(nkb):
            start = pl.multiple_of(((i - jj) * nkb - 1 - d) * bk, bk)
            for h in range(nh):
                o, run = tile(h, start, out[h][1], None)
                out[h] = (out[h][0] + o, run)
        return tuple(out)

    res = lax.fori_loop(0, i, body, tuple(state))
    o_ref[...] = jnp.concatenate([r[0] for r in res], axis=1).astype(o_ref.dtype)


SB_HEADS_PER_STEP = 4
SB_Q_BLOCK = 1024
SB_K_BLOCK = 256


def _sb_attention(u_b, bsz, seq):
    t, w3 = u_b.shape
    width = w3 // 3
    hw = SB_HEADS_PER_STEP * SB_HEAD_DIM
    groups = width // hw
    bq = _tile(seq, SB_Q_BLOCK)
    bk = min(SB_K_BLOCK, bq)
    assert bq % bk == 0
    nq = seq // bq
    return pl.pallas_call(
        functools.partial(_sb_attn_kernel, bq=bq, bk=bk, scale=SB_HEAD_DIM ** -0.5),
        grid=(bsz, groups, nq),
        in_specs=[pl.BlockSpec((bq, hw), lambda b, h, i: (b * nq + i, h)),
                  pl.BlockSpec((seq, hw), lambda b, h, i: (b, groups + h)),
                  pl.BlockSpec((seq, hw), lambda b, h, i: (b, 2 * groups + h))],
        out_specs=pl.BlockSpec((bq, hw), lambda b, h, i: (b * nq + i, h)),
        out_shape=jax.ShapeDtypeStruct((t, width), BF16),
        scratch_shapes=[pltpu.VMEM((SB_HEADS_PER_STEP, bq, SB_HEAD_DIM), BF16),
                        pltpu.VMEM((SB_HEADS_PER_STEP, bq, SB_HEAD_DIM), F32),
                        pltpu.VMEM((SB_HEADS_PER_STEP, bq, LANES), F32)],
        name="sb_attention",
        compiler_params=_params("parallel", "parallel", "arbitrary"),
    )(u_b, u_b, u_b)


def _ssd_kernel(xbc_ref, z_ref, dt_ref, dtb_ref, alog_ref, dsk_ref, nw_ref,
                o_ref, state_ref, acst_ref, *, hpg, hd, lc):
    g = pl.program_id(1)
    gw = hpg * hd
    nsub = xbc_ref.shape[0] // lc
    ns = (xbc_ref.shape[1] - gw) // 2

    @pl.when(pl.program_id(2) == 0)
    def _():
        state_ref[...] = jnp.zeros(state_ref.shape, F32)

    row = lax.broadcasted_iota(I32, (lc, lc), 0)
    col = lax.broadcasted_iota(I32, (lc, lc), 1)
    causal = row >= col
    tri = jnp.where(causal, 1.0, 0.0).astype(BF16)
    hrow = lax.broadcasted_iota(I32, (LANES, gw), 0)
    hcol = lax.broadcasted_iota(I32, (LANES, gw), 1)
    spread = jnp.where(hrow == g * hpg + (hcol >> _log2(hd)), 1.0, 0.0).astype(BF16)
    hrow2 = lax.broadcasted_iota(I32, (LANES, hpg * lc), 0)
    hcol2 = lax.broadcasted_iota(I32, (LANES, hpg * lc), 1)
    spread_l = jnp.where(hrow2 == g * hpg + (hcol2 >> _log2(lc)), 1.0, 0.0).astype(BF16)

    def expand(x, parts, mat):
        pieces = (_split3(x) if parts == 3 else _split2(x) if parts == 2 else (x.astype(BF16),))
        return sum(jnp.dot(p, mat, preferred_element_type=F32) for p in pieces)

    neg_a = -jnp.exp(alog_ref[...])
    dsk_x = expand(jnp.broadcast_to(dsk_ref[...], (SUBLANES, LANES)), 3, spread)[0:1, :]

    def chunk(sc, prev):
        rows = slice(sc * lc, (sc + 1) * lc)
        dt = jax.nn.softplus(dt_ref[rows, :] + dtb_ref[...])
        a_cs = sum(jnp.dot(tri, part, preferred_element_type=F32) for part in _split3(dt * neg_a))
        acst_ref[sc] = a_cs.T
        exp_acs = jnp.exp(a_cs)
        dte = jnp.exp(a_cs[lc - 1:lc, :] - a_cs)

        dt_x = expand(dt, 1, spread)
        dte_x = expand(dte, 1, spread)
        eacs_x = expand(exp_acs, 2, spread)
        acs_col = expand(a_cs, 2, spread_l)

        xs = xbc_ref[rows, 0:gw].astype(F32)
        bm = xbc_ref[rows, gw:gw + ns]
        cm = xbc_ref[rows, gw + ns:gw + 2 * ns]
        xdt = xs * dt_x
        xdt_b = xdt.astype(BF16)
        cb = lax.dot_general(cm, bm, (((1,), (1,)), ((), ())), preferred_element_type=F32)

        y_parts = []
        for r in range(hpg):
            a_l = acs_col[:, r * lc:(r + 1) * lc]
            a_s = acst_ref[sc, pl.ds(g * hpg + r, 1), :]
            seg = jnp.where(causal, jnp.exp(jnp.where(causal, a_l - a_s, 0.0)), 0.0)
            m = (cb * seg).astype(BF16)
            y_parts.append(jnp.dot(m, xdt_b[:, r * hd:(r + 1) * hd], preferred_element_type=F32))
        y_diag = jnp.concatenate(y_parts, axis=1)

        y_off = jnp.dot(cm, prev.astype(BF16), preferred_element_type=F32) * eacs_x
        y = y_diag + y_off + xs * dsk_x

        bm_t = bm.astype(F32).T.astype(BF16)
        new_state = jnp.dot(bm_t, (xdt * dte_x).astype(BF16), preferred_element_type=F32)

        zf = z_ref[rows, :].astype(F32)
        yg = y * (zf * jax.nn.sigmoid(zf))
        ms = jnp.mean(yg * yg, axis=-1, keepdims=True)
        o_ref[rows, :] = (yg * lax.rsqrt(ms + EPS) * nw_ref[...]).astype(o_ref.dtype)
        return prev * eacs_x[lc - 1:lc, :] + new_state

    state = state_ref[...]
    for sc in range(nsub):
        state = chunk(sc, state)
    state_ref[...] = state


def _pad_lanes(v):
    return jnp.zeros((1, LANES), F32).at[0, :v.shape[0]].set(v.astype(F32))


def _ssd_group_major(width):
    gw = width // SSD_GROUPS
    base_b = width
    base_c = width + SSD_GROUPS * SSD_STATE
    cols = []
    for g in range(SSD_GROUPS):
        cols += list(range(g * gw, (g + 1) * gw))
        cols += list(range(base_b + g * SSD_STATE, base_b + (g + 1) * SSD_STATE))
        cols += list(range(base_c + g * SSD_STATE, base_c + (g + 1) * SSD_STATE))
    return jnp.asarray(cols, I32)


def _ssd(xbc_act, z, dt_raw, dt_bias, a_log, d_skip, norm_w, bsz, seq):
    t, width = z.shape
    heads = width // SSD_HEAD_DIM
    hpg = heads // SSD_GROUPS
    gw = hpg * SSD_HEAD_DIM
    nsub = SSD_CHUNKS_PER_STEP if seq % (SSD_CHUNKS_PER_STEP * SSD_CHUNK) == 0 else 1
    lc = nsub * SSD_CHUNK
    nch = seq // lc
    assert gw * SSD_GROUPS == width and SSD_STATE == LANES and heads <= LANES
    row = lambda b, g, c: b * nch + c
    return pl.pallas_call(
        functools.partial(_ssd_kernel, hpg=hpg, hd=SSD_HEAD_DIM, lc=SSD_CHUNK),
        name="ssd",
        grid=(bsz, SSD_GROUPS, nch),
        in_specs=[pl.BlockSpec((lc, gw + 2 * SSD_STATE), lambda b, g, c: (row(b, g, c), g)),
                  pl.BlockSpec((lc, gw), lambda b, g, c: (row(b, g, c), g)),
                  pl.BlockSpec((lc, LANES), lambda b, g, c: (row(b, g, c), 0)),
                  pl.BlockSpec((1, LANES), lambda b, g, c: (0, 0)),
                  pl.BlockSpec((1, LANES), lambda b, g, c: (0, 0)),
                  pl.BlockSpec((1, LANES), lambda b, g, c: (0, 0)),
                  pl.BlockSpec((1, gw), lambda b, g, c: (0, g))],
        out_specs=pl.BlockSpec((lc, gw), lambda b, g, c: (row(b, g, c), g)),
        out_shape=jax.ShapeDtypeStruct((t, width), BF16),
        scratch_shapes=[pltpu.VMEM((SSD_STATE, gw), F32), pltpu.VMEM((nsub, LANES, SSD_CHUNK), F32)],
        compiler_params=_params("parallel", "parallel", "arbitrary"),
    )(xbc_act, z, dt_raw, _pad_lanes(dt_bias), _pad_lanes(a_log),
      _pad_lanes(d_skip), norm_w.reshape(1, width).astype(F32))


def _topk_rows(s, k):
    n = s.shape[0]
    iota = lax.broadcasted_iota(I32, s.shape, 0)
    vals, idxs = [], []
    for it in range(k):
        m = jnp.max(s, axis=0, keepdims=True)
        i = jnp.min(jnp.where(s == m, iota, n), axis=0, keepdims=True)
        vals.append(m)
        idxs.append(i)
        if it + 1 < k:
            s = jnp.where(iota == i, -jnp.inf, s)
    return jnp.concatenate(vals, axis=0), jnp.concatenate(idxs, axis=0)


def _peer_topk_kernel(q_ref, keys_ref, i1_ref, i2_ref, g_ref, *, topk):
    half = q_ref.shape[1] // 2
    sub_v, sub_i = [], []
    for c in range(2):
        s = lax.dot_general(keys_ref[c], q_ref[:, c * half:(c + 1) * half],
                            (((1,), (1,)), ((), ())), preferred_element_type=F32)
        v, i = _topk_rows(s, topk)
        sub_v.append(v)
        sub_i.append(i)
    n_wide = topk // 2
    pieces, starts = [], []
    for a in range(n_wide):
        nb = min(topk, -(-(topk // (a + 1)) // SUBLANES) * SUBLANES)
        starts.append(sum(p.shape[0] for p in pieces))
        pieces.append(sub_v[0][a:a + 1, :] + sub_v[1][0:nb, :])
    tail_start = sum(p.shape[0] for p in pieces)
    pieces.append(sub_v[0][n_wide:topk, :] + sub_v[1][0:1, :])
    top_s, pos = _topk_rows(jnp.concatenate(pieces, axis=0), topk)
    pa = jnp.zeros(pos.shape, I32)
    pb = pos
    for a in range(1, n_wide):
        ge = pos >= starts[a]
        pa = jnp.where(ge, a, pa)
        pb = jnp.where(ge, pos - starts[a], pb)
    ge = pos >= tail_start
    pa = jnp.where(ge, pos + (n_wide - tail_start), pa)
    pb = jnp.where(ge, 0, pb)
    i1 = jnp.zeros(pos.shape, I32)
    i2 = jnp.zeros(pos.shape, I32)
    for a in range(topk):
        i1 = jnp.where(pa == a, sub_i[0][a:a + 1, :], i1)
        i2 = jnp.where(pb == a, sub_i[1][a:a + 1, :], i2)
    e = jnp.exp(top_s - top_s[0:1, :])
    i1_ref[...] = i1
    i2_ref[...] = i2
    g_ref[...] = e / jnp.sum(e, axis=0, keepdims=True)


def _peer_topk(q, sub_keys):
    t, qw = q.shape
    heads, _, nkeys, dk2 = sub_keys.shape
    tt = _tile(t, 1024)
    hk = heads * PEER_TOPK
    out = jax.ShapeDtypeStruct((hk, t), I32)
    spec = pl.BlockSpec((PEER_TOPK, tt), lambda i, h: (h, i))
    return pl.pallas_call(
        functools.partial(_peer_topk_kernel, topk=PEER_TOPK),
        name="peer_topk",
        grid=(t // tt, heads),
        in_specs=[pl.BlockSpec((tt, 2 * dk2), lambda i, h: (i, h)),
                  pl.BlockSpec((None, 2, nkeys, dk2), lambda i, h: (h, 0, 0, 0))],
        out_specs=[spec, spec, spec],
        out_shape=[out, out, jax.ShapeDtypeStruct((hk, t), F32)],
        compiler_params=_params("parallel", "parallel"),
    )(q, sub_keys.astype(BF16))


def _peer_gates_kernel(i1_ref, i2_ref, g_ref, o_ref, i1s_ref, i2s_ref, gs_ref):
    nk = o_ref.shape[2]
    i1s_ref[...] = i1_ref[...].astype(F32).T
    i2s_ref[...] = i2_ref[...].astype(F32).T
    gs_ref[...] = g_ref[...].T
    key = lax.broadcasted_iota(I32, (nk, i1_ref.shape[0]), 0).astype(F32)

    def body(t, _):
        r1 = i1s_ref[pl.ds(t, 1), :]
        r2 = i2s_ref[pl.ds(t, 1), :]
        gg = gs_ref[pl.ds(t, 1), :]
        a_mat = jnp.where(key == r1, gg, 0.0).astype(BF16)
        b_mat = jnp.where(key == r2, 1.0, 0.0).astype(BF16)
        gmap = lax.dot_general(a_mat, b_mat, (((1,), (1,)), ((), ())), preferred_element_type=F32)
        o_ref[:, pl.ds(pl.multiple_of(t * SUBLANES, SUBLANES), SUBLANES), :] = gmap.reshape(
            nk // SUBLANES, SUBLANES, nk)
        return 0

    lax.fori_loop(0, i1s_ref.shape[0], body, 0, unroll=GATES_UNROLL)


def _peer_gates(i1t, i2t, gt, nkeys):
    hk, t = i1t.shape
    tt = _tile(t, 128)
    spec = pl.BlockSpec((hk, tt), lambda i: (0, i))
    return pl.pallas_call(
        _peer_gates_kernel,
        name="peer_gates",
        grid=(t // tt,),
        in_specs=[spec, spec, spec],
        out_specs=pl.BlockSpec((nkeys // SUBLANES, tt * SUBLANES, nkeys), lambda i: (0, i, 0)),
        out_shape=jax.ShapeDtypeStruct((nkeys // SUBLANES, t * SUBLANES, nkeys), F32),
        scratch_shapes=[pltpu.VMEM((tt, hk), F32)] * 3,
        compiler_params=_params("parallel"),
    )(i1t, i2t, gt)


def _peer_ffn_kernel(x_ref, dn_ref, up_ref, gd_ref, o_ref):
    j = pl.program_id(1)
    tm = x_ref.shape[0]
    nk = gd_ref.shape[1]
    per = SUBLANES // 2
    half = o_ref.shape[1] // 2

    @pl.when(j == 0)
    def _():
        o_ref[...] = jnp.zeros(o_ref.shape, F32)

    def step(part):
        s = jnp.dot(x_ref[...], dn_ref[...], preferred_element_type=F32)
        pieces = []
        for c in range(per):
            sc = s[:, c * nk:(c + 1) * nk].astype(BF16)
            act = 0.5 * sc * (1.0 + lax.erf(sc * (2.0 ** -0.5)))
            gate = gd_ref[pl.ds(part * per + c, tm, stride=SUBLANES), :]
            pieces.append(act * gate.astype(BF16))
        a = jnp.concatenate(pieces, axis=1)
        for n in range(2):
            cols = slice(n * half, (n + 1) * half)
            o_ref[:, cols] += jnp.dot(a, up_ref[:, cols], preferred_element_type=F32)

    for part in range(2):
        pl.when(lax.rem(j, 2) == part)(functools.partial(step, part))


def _peer_ffn(n2, down, up, gd3):
    t, d = n2.shape
    ne = up.shape[0]
    nk = gd3.shape[2]
    tm = _tile(t, 1024)
    te = SUBLANES * nk // 2
    nj = ne // te
    assert nj % 2 == 0
    down_t = down.reshape(nj, te, d).transpose(0, 2, 1)
    return pl.pallas_call(
        _peer_ffn_kernel,
        name="peer_ffn",
        grid=(t // tm, nj),
        in_specs=[pl.BlockSpec((tm, d), lambda i, j: (i, 0)),
                  pl.BlockSpec((None, d, te), lambda i, j: (j, 0, 0)),
                  pl.BlockSpec((te, d), lambda i, j: (j, 0)),
                  pl.BlockSpec((None, tm * SUBLANES, nk), lambda i, j: (j // 2, i, 0))],
        out_specs=pl.BlockSpec((tm, d), lambda i, j: (i, 0)),
        out_shape=jax.ShapeDtypeStruct((t, d), F32),
        compiler_params=_params("parallel", "arbitrary", vmem=VMEM_LIMIT_FFN),
    )(n2, down_t, up, gd3)


def _hybrid_mixer(h, n, l, bsz, seq, w_in, conv_a_w, ssd_conv_w, ssd_conv_b, ssd_dt_bias,
                  ssd_a_log, ssd_d, ssd_norm_w, w_branch_a, w_branch_b, w_branch_c, w_out):
    d = h.shape[1]
    conv_w = conv_a_w.shape[2]
    sb_w = w_branch_b.shape[1]
    ssd_w = w_branch_c.shape[1]
    xbc_w = ssd_conv_w.shape[2]
    heads = ssd_d.shape[1]
    bounds = [0, 3 * conv_w, 3 * sb_w, ssd_w, xbc_w, heads, 3 * d]
    offs = [sum(bounds[:k + 1]) for k in range(len(bounds))]
    w = w_in[l]
    w_a, w_b, w_z, w_xbc, w_dt, w_g = (w[:, offs[k]:offs[k + 1]].astype(BF16) for k in range(6))
    order = _ssd_group_major(ssd_w)
    w_xbc = w_xbc[:, order]
    w_dt = jnp.pad(w_dt, ((0, 0), (0, LANES - heads)))

    u_a = _matmul(n, w_a, BF16)
    u_b = _matmul(n, w_b, BF16)
    u_z = _matmul(n, w_z, BF16)
    u_xbc = _matmul(n, w_xbc, BF16)
    u_dt = _matmul(n, w_dt, F32)
    u_g = _matmul(n, w_g, BF16)

    mix_a = _short_conv(u_a, conv_a_w[l], bsz, seq)
    mix_b = _sb_attention(u_b, bsz, seq)
    xbc_act = _ssd_conv(u_xbc, ssd_conv_w[l][:, order], ssd_conv_b[l][order], bsz, seq)
    mix_c = _ssd(xbc_act, u_z, u_dt, ssd_dt_bias[l], ssd_a_log[l], ssd_d[l], ssd_norm_w[l], bsz, seq)

    m = _matmul(mix_a, w_branch_a[l].astype(BF16), F32, gate=u_g)
    m = _matmul(mix_b, w_branch_b[l].astype(BF16), F32, gate=u_g, gate_col=d, add=m)
    m = _matmul(mix_c, w_branch_c[l].astype(BF16), BF16, gate=u_g, gate_col=2 * d, add=m, tn=512)
    return _matmul(m, w_out[l].astype(BF16), F32, add=h)


def _peer(h, l, norm_ffn_w, peer_w_query, peer_sub_keys, peer_down, peer_up):
    n2 = _rmsnorm(h, norm_ffn_w[l], BF16)
    q = _matmul(n2, peer_w_query[l].astype(BF16), BF16)
    i1t, i2t, gt = _peer_topk(q, peer_sub_keys[l])
    gd3 = _peer_gates(i1t, i2t, gt, peer_sub_keys.shape[3])
    return _peer_ffn(n2, peer_down[l].astype(BF16), peer_up[l].astype(BF16), gd3)


def kernel(x, norm_mix_w, w_in, conv_a_w, ssd_conv_w, ssd_conv_b, ssd_dt_bias, ssd_a_log, ssd_d,
           ssd_norm_w, w_branch_a, w_branch_b, w_branch_c, w_out, norm_ffn_w, peer_w_query,
           peer_sub_keys, peer_down, peer_up, final_norm_w):
    bsz, seq, d = x.shape
    h = x.reshape(bsz * seq, d)
    ffn = None
    for l in range(w_in.shape[0]):
        if ffn is None:
            n = _rmsnorm(h, norm_mix_w[l], BF16)
        else:
            h, n = _rmsnorm(h, norm_mix_w[l], BF16, add=ffn)
        h = _hybrid_mixer(h, n, l, bsz, seq, w_in, conv_a_w, ssd_conv_w, ssd_conv_b,
                          ssd_dt_bias, ssd_a_log, ssd_d, ssd_norm_w, w_branch_a, w_branch_b,
                          w_branch_c, w_out)
        ffn = _peer(h, l, norm_ffn_w, peer_w_query, peer_sub_keys, peer_down, peer_up)
    return _rmsnorm(h, final_norm_w, x.dtype, add=ffn, keep_sum=False).reshape(bsz, seq, d)
```
